```python
import jax, jax.numpy as jnp
from jax import lax
import numpy as np

D_MODEL = 4096
BATCH = 2
SEQ = 8192
DEPTH = 1

CHUNK = 64
MIX_WIDTH = D_MODEL
SB_WIDTH = MIX_WIDTH // 2
SB_HEAD_DIM = 128
SB_HEADS = SB_WIDTH // SB_HEAD_DIM
POOL_WIDTH = MIX_WIDTH - SB_WIDTH
POOL_WINDOWS = (2, 4, 8, 16)
POOL_GROUPS = len(POOL_WINDOWS)
POOL_GROUP_DIM = POOL_WIDTH // POOL_GROUPS
IN_WIDTH = 3 * SB_WIDTH + POOL_WIDTH
D_FF = ((8 * D_MODEL // 3 + 255) // 256) * 256
CONV_WIDTH = 3
QUERY_BLOCK = 128
NORM_EPS = 1e-6

kernel_name = "hybrid_stickbreak_pool_convffn_block"


def rmsnorm(x, gain):
    xf = x.astype(jnp.float32)
    y = xf * lax.rsqrt(jnp.mean(xf * xf, axis=-1, keepdims=True) + NORM_EPS)
    return (y * gain.astype(jnp.float32)).astype(x.dtype)


def stick_breaking_attention(q, k, v):
    seq = q.shape[2]
    scale = SB_HEAD_DIM ** -0.5
    outs = []
    for start in range(0, seq, QUERY_BLOCK):
        end = start + QUERY_BLOCK
        qb = q[:, :, start:end]
        kb = k[:, :, :end]
        vb = v[:, :, :end]
        z = jnp.einsum('bhqd,bhkd->bhqk', qb, kb) * scale
        t_pos = start + jnp.arange(QUERY_BLOCK)[:, None]
        s_pos = jnp.arange(end)[None, :]
        before = s_pos < t_pos
        log_beta = jax.nn.log_sigmoid(z)
        log_keep = jnp.where(before, log_beta - z, 0.0)
        suffix = lax.cumsum(log_keep, axis=3, reverse=True) - log_keep
        weights = jnp.where(before, jnp.exp(log_beta + suffix), 0.0)
        outs.append(jnp.einsum('bhqk,bhkd->bhqd', weights, vb))
    return jnp.concatenate(outs, axis=2)


def multiscale_pool(u, pool_w, pool_scale):
    b, s, _ = u.shape
    uf = u.astype(jnp.float32).reshape(b, s, POOL_GROUPS, POOL_GROUP_DIM)
    cs = jnp.cumsum(uf, axis=1)
    pos = jnp.arange(1, s + 1, dtype=jnp.float32)
    groups = []
    for g, w in enumerate(POOL_WINDOWS):
        c = cs[:, :, g]
        lagged = jnp.pad(c, ((0, 0), (w, 0), (0, 0)))[:, :s]
        count = jnp.minimum(pos, jnp.float32(w))[None, :, None]
        groups.append((c - lagged) / count - uf[:, :, g])
    pooled = jnp.stack(groups, axis=2).astype(u.dtype)
    mixed = jnp.einsum('bsgc,gcd->bsgd', pooled, pool_w)
    return mixed.reshape(b, s, POOL_WIDTH) * pool_scale


def token_mixer(h, w_in, pool_w, pool_scale, w_out):
    b, s, _ = h.shape
    proj = h @ w_in
    q = proj[..., 0 * SB_WIDTH:1 * SB_WIDTH]
    k = proj[..., 1 * SB_WIDTH:2 * SB_WIDTH]
    v = proj[..., 2 * SB_WIDTH:3 * SB_WIDTH]
    u = proj[..., 3 * SB_WIDTH:]

    def heads(t):
        return t.astype(jnp.float32).reshape(b, s, SB_HEADS, SB_HEAD_DIM).transpose(0, 2, 1, 3)

    attn = stick_breaking_attention(heads(q), heads(k), heads(v))
    attn = attn.transpose(0, 2, 1, 3).reshape(b, s, SB_WIDTH).astype(h.dtype)
    pooled = multiscale_pool(u, pool_w, pool_scale).astype(h.dtype)
    return jnp.concatenate([attn, pooled], axis=-1) @ w_out


def conv_gated_mlp(h, w_up, conv_w, conv_b, w_down):
    up = h @ w_up
    s = up.shape[1]
    padded = jnp.pad(up, ((0, 0), (CONV_WIDTH - 1, 0), (0, 0)))
    conv = conv_b + conv_w[0] * padded[:, 0:s]
    for i in range(1, CONV_WIDTH):
        conv = conv + conv_w[i] * padded[:, i:i + s]
    gate, value = jnp.split(conv, 2, axis=-1)
    return (jax.nn.gelu(gate, approximate=True) * value) @ w_down


def setup_inputs(seed: int = 0) -> dict:
    key = jax.random.key(seed)
    ks = jax.random.split(key, 14)
    f32 = jnp.float32

    def normal(k, shape, scale):
        return jax.random.normal(k, shape, f32) * scale

    def gain(k):
        return 1.0 + normal(k, (DEPTH, D_MODEL), 0.02)

    return {
        "x": normal(ks[0], (BATCH, SEQ, D_MODEL), 1.0),
        "pre_mix_norm": gain(ks[1]),
        "w_in": normal(ks[2], (DEPTH, D_MODEL, IN_WIDTH), D_MODEL ** -0.5),
        "pool_w": normal(ks[3], (DEPTH, POOL_GROUPS, POOL_GROUP_DIM, POOL_GROUP_DIM), POOL_GROUP_DIM ** -0.5),
        "pool_scale": 1.0 + normal(ks[4], (DEPTH, POOL_WIDTH), 0.02),
        "w_out": normal(ks[5], (DEPTH, MIX_WIDTH, D_MODEL), MIX_WIDTH ** -0.5),
        "post_mix_norm": gain(ks[6]),
        "pre_ffn_norm": gain(ks[7]),
        "w_up": normal(ks[8], (DEPTH, D_MODEL, 2 * D_FF), D_MODEL ** -0.5),
        "conv_w": normal(ks[9], (DEPTH, CONV_WIDTH, 2 * D_FF), CONV_WIDTH ** -0.5),
        "conv_b": normal(ks[10], (DEPTH, 2 * D_FF), 0.01),
        "w_down": normal(ks[11], (DEPTH, D_FF, D_MODEL), D_FF ** -0.5),
        "post_ffn_norm": gain(ks[12]),
    }


def reference(x, pre_mix_norm, w_in, pool_w, pool_scale, w_out, post_mix_norm,
              pre_ffn_norm, w_up, conv_w, conv_b, w_down, post_ffn_norm):
    for l in range(DEPTH):
        h = rmsnorm(x, pre_mix_norm[l])
        mix = token_mixer(h, w_in[l], pool_w[l], pool_scale[l], w_out[l])
        x = x + rmsnorm(mix, post_mix_norm[l])
        h = rmsnorm(x, pre_ffn_norm[l])
        ffn = conv_gated_mlp(h, w_up[l], conv_w[l], conv_b[l], w_down[l])
        x = x + rmsnorm(ffn, post_ffn_norm[l])
    return x
```

```python
import functools
import math

import jax
import jax.numpy as jnp
from jax import lax
from jax.experimental import pallas as pl
from jax.experimental.pallas import tpu as pltpu

F32 = jnp.float32
BF16 = jnp.bfloat16

D_MODEL = 4096
SB_WIDTH = 2048
HEAD_DIM = 128
HEADS = SB_WIDTH // HEAD_DIM
POOL_WIDTH = 2048
POOL_WINDOWS = (2, 4, 8, 16)
POOL_GROUP_DIM = POOL_WIDTH // len(POOL_WINDOWS)
D_FF = 11008
CONV_TAPS = 3
NORM_EPS = 1e-6

LANES = 128
SUBLANES_F32 = 8
SUBLANES_BF16 = 16
MIB = 1024 * 1024
COMPILER_SCRATCH_BYTES = 6 * MIB

FF_TILE = 512
D_FF_PAD = ((D_FF + 1023) // 1024) * 1024
DOWN_TK = 1024

ATTN_TQ = 256
ATTN_TK = 128


def _params(semantics, vmem_bytes):
    return pltpu.CompilerParams(dimension_semantics=semantics,
                                vmem_limit_bytes=int(vmem_bytes))


def _nbytes(shape, dtype):
    return math.prod(shape) * jnp.dtype(dtype).itemsize


def _vmem_estimate(pipelined, scratch=(), temps=()):
    total = 2 * sum(_nbytes(s, d) for s, d in pipelined)
    total += sum(_nbytes(s, d) for s, d in scratch)
    total += sum(_nbytes(s, d) for s, d in temps)
    return total + COMPILER_SCRATCH_BYTES


def _rms(x, gain):
    return x * lax.rsqrt(jnp.mean(x * x, axis=-1, keepdims=True) + NORM_EPS) * gain


def _prenorm_kernel(x_ref, g_ref, o_ref):
    o_ref[...] = _rms(x_ref[...], g_ref[...]).astype(o_ref.dtype)


def _prenorm(x, gain, tm=256):
    t, d = x.shape
    return pl.pallas_call(
        _prenorm_kernel,
        grid=(t // tm,),
        in_specs=[pl.BlockSpec((tm, d), lambda i: (i, 0)),
                  pl.BlockSpec((1, d), lambda i: (0, 0))],
        out_specs=pl.BlockSpec((tm, d), lambda i: (i, 0)),
        out_shape=jax.ShapeDtypeStruct((t, d), BF16),
        compiler_params=_params(("arbitrary",), _vmem_estimate(
            [((tm, d), F32), ((tm, d), BF16)], temps=[((tm, d), F32)])),
        name="prenorm",
    )(x, gain)


def _post_mix_kernel(x_ref, m_ref, g1_ref, g2_ref, x1_ref, h2_ref):
    x1 = x_ref[...] + _rms(m_ref[...], g1_ref[...])
    x1_ref[...] = x1
    h2_ref[...] = _rms(x1, g2_ref[...]).astype(h2_ref.dtype)


def _post_mix(x, mix, g1, g2, tm=256):
    t, d = x.shape
    row = pl.BlockSpec((tm, d), lambda i: (i, 0))
    vec = pl.BlockSpec((1, d), lambda i: (0, 0))
    return pl.pallas_call(
        _post_mix_kernel,
        grid=(t // tm,),
        in_specs=[row, row, vec, vec],
        out_specs=[row, row],
        out_shape=[jax.ShapeDtypeStruct((t, d), F32), jax.ShapeDtypeStruct((t, d), BF16)],
        compiler_params=_params(("arbitrary",), _vmem_estimate(
            [((tm, d), F32)] * 3 + [((tm, d), BF16)], temps=[((tm, d), F32)] * 2)),
        name="post_mix",
    )(x, mix, g1, g2)


def _post_ffn_kernel(x_ref, f_ref, g_ref, o_ref):
    o_ref[...] = x_ref[...] + _rms(f_ref[...], g_ref[...])


def _post_ffn(x1, ffn, g, tm=256):
    t, d = x1.shape
    row = pl.BlockSpec((tm, d), lambda i: (i, 0))
    vec = pl.BlockSpec((1, d), lambda i: (0, 0))
    return pl.pallas_call(
        _post_ffn_kernel,
        grid=(t // tm,),
        in_specs=[row, row, vec],
        out_specs=row,
        out_shape=jax.ShapeDtypeStruct((t, d), F32),
        compiler_params=_params(("arbitrary",), _vmem_estimate(
            [((tm, d), F32)] * 3, temps=[((tm, d), F32)])),
        name="post_ffn",
    )(x1, ffn, g)


def _in_proj_kernel(a_ref, w_ref, o_ref):
    o_ref[...] = jnp.dot(a_ref[...], w_ref[...],
                         preferred_element_type=F32).astype(o_ref.dtype)


def _in_proj(h, w, tm=1024, tn=1024):
    t, k = h.shape
    n = w.shape[1]
    return pl.pallas_call(
        _in_proj_kernel,
        grid=(t // tm, n // tn),
        in_specs=[pl.BlockSpec((tm, k), lambda i, j: (i, 0)),
                  pl.BlockSpec((k, tn), lambda i, j: (0, j))],
        out_specs=pl.BlockSpec((tm, tn), lambda i, j: (i, j)),
        out_shape=jax.ShapeDtypeStruct((t, n), BF16),
        compiler_params=_params(("arbitrary", "arbitrary"), _vmem_estimate(
            [((tm, k), BF16), ((k, tn), BF16), ((tm, tn), BF16)],
            temps=[((tm, tn), F32)])),
        name="in_proj",
    )(h, w)


def _out_proj_kernel(a_ref, p_ref, wa_ref, wp_ref, o_ref):
    acc = jnp.dot(a_ref[...], wa_ref[...], preferred_element_type=F32)
    acc += jnp.dot(p_ref[...], wp_ref[...], preferred_element_type=F32)
    o_ref[...] = acc


def _out_proj(attn, pooled, w, tm=1024, tn=1024):
    t, ka = attn.shape
    kp = pooled.shape[1]
    assert ka == kp
    n = w.shape[1]
    return pl.pallas_call(
        _out_proj_kernel,
        grid=(t // tm, n // tn),
        in_specs=[pl.BlockSpec((tm, ka), lambda i, j: (i, 0)),
                  pl.BlockSpec((tm, kp), lambda i, j: (i, 0)),
                  pl.BlockSpec((ka, tn), lambda i, j: (0, j)),
                  pl.BlockSpec((kp, tn), lambda i, j: (1, j))],
        out_specs=pl.BlockSpec((tm, tn), lambda i, j: (i, j)),
        out_shape=jax.ShapeDtypeStruct((t, n), F32),
        compiler_params=_params(("arbitrary", "arbitrary"), _vmem_estimate(
            [((tm, ka), BF16), ((tm, kp), BF16), ((ka, tn), BF16), ((kp, tn), BF16),
             ((tm, tn), F32)], temps=[((tm, tn), F32)])),
        name="out_proj",
    )(attn, pooled, w, w)


def _down_proj_kernel(a_ref, w_ref, o_ref):
    @pl.when(pl.program_id(2) == 0)
    def _():
        o_ref[...] = jnp.zeros_like(o_ref)

    o_ref[...] += jnp.dot(a_ref[...], w_ref[...], preferred_element_type=F32)


def _down_proj(act, w, tm=1024, tn=2048, tk=DOWN_TK):
    t, k = act.shape
    n = w.shape[1]
    return pl.pallas_call(
        _down_proj_kernel,
        grid=(t // tm, n // tn, k // tk),
        in_specs=[pl.BlockSpec((tm, tk), lambda i, j, kk: (i, kk)),
                  pl.BlockSpec((tk, tn), lambda i, j, kk: (kk, j))],
        out_specs=pl.BlockSpec((tm, tn), lambda i, j, kk: (i, j)),
        out_shape=jax.ShapeDtypeStruct((t, n), F32),
        compiler_params=_params(("arbitrary",) * 3, _vmem_estimate(
            [((tm, tk), BF16), ((tk, tn), BF16), ((tm, tn), F32)],
            temps=[((tm, tn), F32)])),
        name="down_proj",
    )(act, w)


def _gelu_tanh(x):
    c = math.sqrt(2.0 / math.pi)
    return 0.5 * x * (1.0 + jnp.tanh(c * (x + 0.044715 * (x * x * x))))


def _causal_conv(u, prev, cw_ref, cb_ref):
    rows = u.shape[0]
    head = SUBLANES_BF16
    w0 = cw_ref[0:1, :]
    w1 = cw_ref[1:2, :]
    w2 = cw_ref[2:3, :]
    b = cb_ref[...]
    full = b + w2 * u + w1 * pltpu.roll(u, 1, axis=0) + w0 * pltpu.roll(u, 2, axis=0)
    uh = u[:head]
    ph = jnp.concatenate([prev, prev], axis=0)
    r = lax.broadcasted_iota(jnp.int32, uh.shape, 0)
    u1 = jnp.where(r < 1, pltpu.roll(ph, 1, axis=0), pltpu.roll(uh, 1, axis=0))
    u2 = jnp.where(r < 2, pltpu.roll(ph, 2, axis=0), pltpu.roll(uh, 2, axis=0))
    first = b + w2 * uh + w1 * u1 + w0 * u2
    del rows
    return full, first


def _ffn_up_kernel(h_ref, wg_ref, wv_ref, cwg_ref, cwv_ref, cbg_ref, cbv_ref,
                   o_ref, carry_ref, *, tiles_per_seq):
    i = pl.program_id(0)
    j = pl.program_id(1)
    tm = h_ref.shape[0]
    head = SUBLANES_BF16
    h = h_ref[...]
    ug = jnp.dot(h, wg_ref[...], preferred_element_type=F32)
    uv = jnp.dot(h, wv_ref[...], preferred_element_type=F32)
    seq_start = (i % tiles_per_seq) == 0
    pg = jnp.where(seq_start, 0.0, carry_ref[j, 0])
    pv = jnp.where(seq_start, 0.0, carry_ref[j, 1])
    carry_ref[j, 0] = ug[tm - SUBLANES_F32:, :]
    carry_ref[j, 1] = uv[tm - SUBLANES_F32:, :]
    g_full, g_first = _causal_conv(ug, pg, cwg_ref, cbg_ref)
    v_full, v_first = _causal_conv(uv, pv, cwv_ref, cbv_ref)
    o_ref[...] = (_gelu_tanh(g_full) * v_full).astype(o_ref.dtype)
    o_ref[0:head, :] = (_gelu_tanh(g_first) * v_first).astype(o_ref.dtype)


def _ffn_up(h2, wg, wv, cwg, cwv, cbg, cbv, seq, tm=1024, tf=FF_TILE):
    t, k = h2.shape
    n = wg.shape[1]
    nj = n // tf
    wspec = pl.BlockSpec((k, tf), lambda i, j: (0, j))
    cwspec = pl.BlockSpec((CONV_TAPS, tf), lambda i, j: (0, j))
    cbspec = pl.BlockSpec((1, tf), lambda i, j: (0, j))
    return pl.pallas_call(
        functools.partial(_ffn_up_kernel, tiles_per_seq=seq // tm),
        grid=(t // tm, nj),
        in_specs=[pl.BlockSpec((tm, k), lambda i, j: (i, 0)),
                  wspec, wspec, cwspec, cwspec, cbspec, cbspec],
        out_specs=pl.BlockSpec((tm, tf), lambda i, j: (i, j)),
        out_shape=jax.ShapeDtypeStruct((t, n), BF16),
        scratch_shapes=[pltpu.VMEM((nj, 2, SUBLANES_F32, tf), F32)],
        compiler_params=_params(("arbitrary", "arbitrary"), _vmem_estimate(
            [((tm, k), BF16), ((k, tf), BF16), ((k, tf), BF16), ((tm, tf), BF16)],
            scratch=[((nj, 2, SUBLANES_F32, tf), F32)],
            temps=[((tm, tf), F32)] * 6)),
        name="ffn_up",
    )(h2, wg, wv, cwg, cwv, cbg, cbv)


def _pool_kernel(u_ref, halo_ref, pw_ref, ps_ref, o_ref, *, tiles_per_seq):
    i = pl.program_id(0)
    tm = u_ref.shape[0]
    halo = halo_ref.shape[0]
    it = i % tiles_per_seq
    seq_start = it == 0
    pos = (it * tm + 1 + lax.broadcasted_iota(jnp.int32, (tm, 1), 0)).astype(F32)
    gd = POOL_GROUP_DIM
    for g, window in enumerate(POOL_WINDOWS):
        cols = slice(g * gd, (g + 1) * gd)
        u = u_ref[:, cols].astype(F32)
        hl = jnp.where(seq_start, 0.0, halo_ref[:, cols].astype(F32))
        s = jnp.concatenate([hl, u], axis=0)
        shift = 1
        while shift < window:
            s = s + pltpu.roll(s, shift, axis=0)
            shift *= 2
        count = jnp.minimum(pos, float(window))
        pooled = s[halo:, :] / count - u
        mixed = jnp.dot(pooled.astype(BF16), pw_ref[g], preferred_element_type=F32)
        o_ref[:, cols] = (mixed * ps_ref[:, cols]).astype(o_ref.dtype)


def _pool(proj, pool_w, pool_scale, seq, tm=512):
    t = proj.shape[0]
    halo = SUBLANES_BF16
    assert max(POOL_WINDOWS) <= halo
    col_block = (proj.shape[1] - POOL_WIDTH) // POOL_WIDTH
    r = tm // halo
    ng = len(POOL_WINDOWS)
    return pl.pallas_call(
        functools.partial(_pool_kernel, tiles_per_seq=seq // tm),
        grid=(t // tm,),
        in_specs=[pl.BlockSpec((tm, POOL_WIDTH), lambda i: (i, col_block)),
                  pl.BlockSpec((halo, POOL_WIDTH),
                               lambda i: (jnp.maximum(i * r - 1, 0), col_block)),
                  pl.BlockSpec((ng, POOL_GROUP_DIM, POOL_GROUP_DIM), lambda i: (0, 0, 0)),
                  pl.BlockSpec((1, POOL_WIDTH), lambda i: (0, 0))],
        out_specs=pl.BlockSpec((tm, POOL_WIDTH), lambda i: (i, 0)),
        out_shape=jax.ShapeDtypeStruct((t, POOL_WIDTH), BF16),
        compiler_params=_params(("arbitrary",), _vmem_estimate(
            [((tm, POOL_WIDTH), BF16), ((halo, POOL_WIDTH), BF16),
             ((ng, POOL_GROUP_DIM, POOL_GROUP_DIM), BF16), ((tm, POOL_WIDTH), BF16)],
            temps=[((tm + halo, POOL_GROUP_DIM), F32)] * 6)),
        name="pool_mixer",
    )(proj, proj, pool_w, pool_scale)


def _attn_kernel(q_ref, k_ref, v_ref, o_ref, vt_ref, acc_ref, *, tq, tk):
    qi = pl.program_id(1)
    seq = k_ref.shape[0]
    nkb = seq // tk
    scale = HEAD_DIM ** -0.5

    @pl.when(qi == 0)
    def _():
        def xpose(c, _):
            blk = v_ref[pl.ds(pl.multiple_of(c * tk, tk), tk), :].astype(F32)
            vt_ref[c] = blk.T.astype(BF16)
            return 0
        lax.fori_loop(0, nkb, xpose, 0)

    q = (q_ref[...].astype(F32) * scale).astype(BF16)
    rr = lax.broadcasted_iota(jnp.int32, (tk, 2 * tk), 0)
    cc = lax.broadcasted_iota(jnp.int32, (tk, 2 * tk), 1)
    tri2 = jnp.where((cc % tk) >= rr, 1.0, 0.0).astype(BF16)
    s_loc = lax.broadcasted_iota(jnp.int32, (tk, tq), 0)
    t_loc = lax.broadcasted_iota(jnp.int32, (tk, tq), 1)

    acc_ref[...] = jnp.zeros_like(acc_ref)

    def tile(j, c, masked):
        kblk = k_ref[pl.ds(pl.multiple_of(j * tk, tk), tk), :]
        zt = lax.dot_general(kblk, q, (((1,), (1,)), ((), ())),
                             preferred_element_type=F32)
        p = jnp.maximum(zt, 0.0) + jnp.log1p(jnp.exp(-jnp.abs(zt)))
        if masked:
            valid = (j * tk + s_loc) < (qi * tq + t_loc)
            p = jnp.where(valid, p, 0.0)
        hi = p.astype(BF16)
        lo = (p - hi.astype(F32)).astype(BF16)
        st = jnp.dot(tri2, jnp.concatenate([hi, lo], axis=0),
                     preferred_element_type=F32)
        w = jnp.exp(zt - st - c)
        if masked:
            w = jnp.where(valid, w, 0.0)
        acc_ref[...] += jnp.dot(vt_ref[j], w.astype(BF16), preferred_element_type=F32)
        return c + st[0:1, :]

    n_diag = tq // tk
    n_full = qi * n_diag
    c = jnp.zeros((1, tq), F32)
    for m in range(n_diag):
        c = tile(n_full + (n_diag - 1 - m), c, True)
    c = lax.fori_loop(0, n_full, lambda it, cc_: tile(n_full - 1 - it, cc_, False), c)
    o_ref[...] = acc_ref[...].T.astype(o_ref.dtype)


def _attention(proj, batch, seq, tq=ATTN_TQ, tk=ATTN_TK):
    t = proj.shape[0]
    nq = seq // tq

    def bh(g):
        return g // HEADS, g % HEADS

    def q_map(g, qi):
        b, h = bh(g)
        return b * nq + qi, h

    def k_map(g, qi):
        b, h = bh(g)
        return b, HEADS + h

    def v_map(g, qi):
        b, h = bh(g)
        return b, 2 * HEADS + h

    return pl.pallas_call(
        functools.partial(_attn_kernel, tq=tq, tk=tk),
        grid=(batch * HEADS, nq),
        in_specs=[pl.BlockSpec((tq, HEAD_DIM), q_map),
                  pl.BlockSpec((seq, HEAD_DIM), k_map),
                  pl.BlockSpec((seq, HEAD_DIM), v_map)],
        out_specs=pl.BlockSpec((tq, HEAD_DIM), q_map),
        out_shape=jax.ShapeDtypeStruct((t, SB_WIDTH), BF16),
        scratch_shapes=[pltpu.VMEM((seq // tk, HEAD_DIM, tk), BF16),
                        pltpu.VMEM((HEAD_DIM, tq), F32)],
        compiler_params=_params(("arbitrary", "arbitrary"), _vmem_estimate(
            [((tq, HEAD_DIM), BF16), ((seq, HEAD_DIM), BF16), ((seq, HEAD_DIM), BF16),
             ((tq, HEAD_DIM), BF16)],
            scratch=[((seq, HEAD_DIM), BF16), ((HEAD_DIM, tq), F32)],
            temps=[((tk, tq), F32)] * 8)),
        name="stickbreak_attn",
    )(proj, proj, proj)


def _pad_cols(a, n):
    return jnp.pad(a, ((0, 0), (0, n - a.shape[1])))


def _layer(x, pre_mix_norm, w_in, pool_w, pool_scale, w_out, post_mix_norm,
           pre_ffn_norm, w_up, conv_w, conv_b, w_down, post_ffn_norm, batch, seq):
    vec = lambda g: g.reshape(1, -1).astype(F32)
    w_in_b = w_in.astype(BF16)
    w_out_b = w_out.astype(BF16)
    pool_w_b = pool_w.astype(BF16)
    wg = _pad_cols(w_up[:, :D_FF].astype(BF16), D_FF_PAD)
    wv = _pad_cols(w_up[:, D_FF:].astype(BF16), D_FF_PAD)
    cwg = _pad_cols(conv_w[:, :D_FF], D_FF_PAD)
    cwv = _pad_cols(conv_w[:, D_FF:], D_FF_PAD)
    cbg = _pad_cols(conv_b[None, :D_FF], D_FF_PAD)
    cbv = _pad_cols(conv_b[None, D_FF:], D_FF_PAD)
    w_down_b = jnp.pad(w_down.astype(BF16), ((0, D_FF_PAD - D_FF), (0, 0)))

    h = _prenorm(x, vec(pre_mix_norm))
    proj = _in_proj(h, w_in_b)
    attn = _attention(proj, batch, seq)
    pooled = _pool(proj, pool_w_b, vec(pool_scale), seq)
    mix = _out_proj(attn, pooled, w_out_b)
    x1, h2 = _post_mix(x, mix, vec(post_mix_norm), vec(pre_ffn_norm))
    act = _ffn_up(h2, wg, wv, cwg, cwv, cbg, cbv, seq)
    ffn = _down_proj(act, w_down_b)
    return _post_ffn(x1, ffn, vec(post_ffn_norm))


def kernel(x, pre_mix_norm, w_in, pool_w, pool_scale, w_out, post_mix_norm,
           pre_ffn_norm, w_up, conv_w, conv_b, w_down, post_ffn_norm):
    batch, seq, d = x.shape
    assert d == D_MODEL and w_up.shape[-1] == 2 * D_FF
    xt = x.reshape(batch * seq, d)
    for l in range(pre_mix_norm.shape[0]):
        xt = _layer(xt, pre_mix_norm[l], w_in[l], pool_w[l], pool_scale[l], w_out[l],
                    post_mix_norm[l], pre_ffn_norm[l], w_up[l], conv_w[l], conv_b[l],
                    w_down[l], post_ffn_norm[l], batch, seq)
    return xt.reshape(batch, seq, d)
```

```python
import functools
import math

import jax
import jax.numpy as jnp
from jax import lax
from jax.experimental import pallas as pl
from jax.experimental.pallas import tpu as pltpu

F32 = jnp.float32
BF16 = jnp.bfloat16

D_MODEL = 4096
SB_WIDTH = 2048
HEAD_DIM = 128
HEADS = SB_WIDTH // HEAD_DIM
POOL_WIDTH = 2048
POOL_WINDOWS = (2, 4, 8, 16)
POOL_GROUP_DIM = POOL_WIDTH // len(POOL_WINDOWS)
D_FF = 11008
CONV_TAPS = 3
NORM_EPS = 1e-6

LANES = 128
SUBLANES_F32 = 8
SUBLANES_BF16 = 16
MIB = 1024 * 1024
COMPILER_SCRATCH_BYTES = 6 * MIB

FF_TILE = 512
D_FF_PAD = ((D_FF + 1023) // 1024) * 1024
DOWN_TK = 1024

ATTN_TK = 128
ATTN_UNROLL = 4
ATTN_HEADS_PER_STEP = 4
LOG2E = math.log2(math.e)


def _params(semantics, vmem_bytes):
    return pltpu.CompilerParams(dimension_semantics=semantics,
                                vmem_limit_bytes=int(vmem_bytes))


def _nbytes(shape, dtype):
    return math.prod(shape) * jnp.dtype(dtype).itemsize


def _vmem_estimate(pipelined, scratch=(), temps=()):
    total = 2 * sum(_nbytes(s, d) for s, d in pipelined)
    total += sum(_nbytes(s, d) for s, d in scratch)
    total += sum(_nbytes(s, d) for s, d in temps)
    return total + COMPILER_SCRATCH_BYTES


def _rms(x, gain):
    return x * lax.rsqrt(jnp.mean(x * x, axis=-1, keepdims=True) + NORM_EPS) * gain


def _prenorm_kernel(x_ref, g_ref, o_ref):
    o_ref[...] = _rms(x_ref[...], g_ref[...]).astype(o_ref.dtype)


def _prenorm(x, gain, tm=256):
    t, d = x.shape
    return pl.pallas_call(
        _prenorm_kernel,
        grid=(t // tm,),
        in_specs=[pl.BlockSpec((tm, d), lambda i: (i, 0)),
                  pl.BlockSpec((1, d), lambda i: (0, 0))],
        out_specs=pl.BlockSpec((tm, d), lambda i: (i, 0)),
        out_shape=jax.ShapeDtypeStruct((t, d), BF16),
        compiler_params=_params(("arbitrary",), _vmem_estimate(
            [((tm, d), F32), ((tm, d), BF16)], temps=[((tm, d), F32)])),
        name="prenorm",
    )(x, gain)


def _post_mix_kernel(x_ref, m_ref, g1_ref, g2_ref, x1_ref, h2_ref):
    x1 = x_ref[...] + _rms(m_ref[...], g1_ref[...])
    x1_ref[...] = x1
    h2_ref[...] = _rms(x1, g2_ref[...]).astype(h2_ref.dtype)


def _post_mix(x, mix, g1, g2, tm=256):
    t, d = x.shape
    row = pl.BlockSpec((tm, d), lambda i: (i, 0))
    vec = pl.BlockSpec((1, d), lambda i: (0, 0))
    return pl.pallas_call(
        _post_mix_kernel,
        grid=(t // tm,),
        in_specs=[row, row, vec, vec],
        out_specs=[row, row],
        out_shape=[jax.ShapeDtypeStruct((t, d), F32), jax.ShapeDtypeStruct((t, d), BF16)],
        compiler_params=_params(("arbitrary",), _vmem_estimate(
            [((tm, d), F32)] * 3 + [((tm, d), BF16)], temps=[((tm, d), F32)] * 2)),
        name="post_mix",
    )(x, mix, g1, g2)


def _post_ffn_kernel(x_ref, f_ref, g_ref, o_ref):
    o_ref[...] = x_ref[...] + _rms(f_ref[...], g_ref[...])


def _post_ffn(x1, ffn, g, tm=256):
    t, d = x1.shape
    row = pl.BlockSpec((tm, d), lambda i: (i, 0))
    vec = pl.BlockSpec((1, d), lambda i: (0, 0))
    return pl.pallas_call(
        _post_ffn_kernel,
        grid=(t // tm,),
        in_specs=[row, row, vec],
        out_specs=row,
        out_shape=jax.ShapeDtypeStruct((t, d), F32),
        compiler_params=_params(("arbitrary",), _vmem_estimate(
            [((tm, d), F32)] * 3, temps=[((tm, d), F32)])),
        name="post_ffn",
    )(x1, ffn, g)


def _in_proj_kernel(a_ref, w_ref, o_ref):
    o_ref[...] = jnp.dot(a_ref[...], w_ref[...],
                         preferred_element_type=F32).astype(o_ref.dtype)


def _in_proj(h, w, tm=1024, tn=1024):
    t, k = h.shape
    n = w.shape[1]
    return pl.pallas_call(
        _in_proj_kernel,
        grid=(t // tm, n // tn),
        in_specs=[pl.BlockSpec((tm, k), lambda i, j: (i, 0)),
                  pl.BlockSpec((k, tn), lambda i, j: (0, j))],
        out_specs=pl.BlockSpec((tm, tn), lambda i, j: (i, j)),
        out_shape=jax.ShapeDtypeStruct((t, n), BF16),
        compiler_params=_params(("arbitrary", "arbitrary"), _vmem_estimate(
            [((tm, k), BF16), ((k, tn), BF16), ((tm, tn), BF16)],
            temps=[((tm, tn), F32)])),
        name="in_proj",
    )(h, w)


def _out_proj_kernel(a_ref, p_ref, wa_ref, wp_ref, o_ref):
    acc = jnp.dot(a_ref[...], wa_ref[...], preferred_element_type=F32)
    acc += jnp.dot(p_ref[...], wp_ref[...], preferred_element_type=F32)
    o_ref[...] = acc


def _out_proj(attn, pooled, w, tm=1024, tn=1024):
    t, ka = attn.shape
    kp = pooled.shape[1]
    assert ka == kp
    n = w.shape[1]
    return pl.pallas_call(
        _out_proj_kernel,
        grid=(t // tm, n // tn),
        in_specs=[pl.BlockSpec((tm, ka), lambda i, j: (i, 0)),
                  pl.BlockSpec((tm, kp), lambda i, j: (i, 0)),
                  pl.BlockSpec((ka, tn), lambda i, j: (0, j)),
                  pl.BlockSpec((kp, tn), lambda i, j: (1, j))],
        out_specs=pl.BlockSpec((tm, tn), lambda i, j: (i, j)),
        out_shape=jax.ShapeDtypeStruct((t, n), F32),
        compiler_params=_params(("arbitrary", "arbitrary"), _vmem_estimate(
            [((tm, ka), BF16), ((tm, kp), BF16), ((ka, tn), BF16), ((kp, tn), BF16),
             ((tm, tn), F32)], temps=[((tm, tn), F32)])),
        name="out_proj",
    )(attn, pooled, w, w)


def _down_proj_kernel(a_ref, w_ref, o_ref):
    @pl.when(pl.program_id(2) == 0)
    def _():
        o_ref[...] = jnp.zeros_like(o_ref)

    o_ref[...] += jnp.dot(a_ref[...], w_ref[...], preferred_element_type=F32)


def _down_proj(act, w, tm=1024, tn=2048, tk=DOWN_TK):
    t, k = act.shape
    n = w.shape[1]
    return pl.pallas_call(
        _down_proj_kernel,
        grid=(t // tm, n // tn, k // tk),
        in_specs=[pl.BlockSpec((tm, tk), lambda i, j, kk: (i, kk)),
                  pl.BlockSpec((tk, tn), lambda i, j, kk: (kk, j))],
        out_specs=pl.BlockSpec((tm, tn), lambda i, j, kk: (i, j)),
        out_shape=jax.ShapeDtypeStruct((t, n), F32),
        compiler_params=_params(("arbitrary",) * 3, _vmem_estimate(
            [((tm, tk), BF16), ((tk, tn), BF16), ((tm, tn), F32)],
            temps=[((tm, tn), F32)])),
        name="down_proj",
    )(act, w)


def _gelu_tanh(x):
    c = math.sqrt(2.0 / math.pi)
    return 0.5 * x * (1.0 + jnp.tanh(c * (x + 0.044715 * (x * x * x))))


def _causal_conv(u, prev, cw_ref, cb_ref):
    rows = u.shape[0]
    head = SUBLANES_BF16
    w0 = cw_ref[0:1, :]
    w1 = cw_ref[1:2, :]
    w2 = cw_ref[2:3, :]
    b = cb_ref[...]
    full = b + w2 * u + w1 * pltpu.roll(u, 1, axis=0) + w0 * pltpu.roll(u, 2, axis=0)
    uh = u[:head]
    ph = jnp.concatenate([prev, prev], axis=0)
    r = lax.broadcasted_iota(jnp.int32, uh.shape, 0)
    u1 = jnp.where(r < 1, pltpu.roll(ph, 1, axis=0), pltpu.roll(uh, 1, axis=0))
    u2 = jnp.where(r < 2, pltpu.roll(ph, 2, axis=0), pltpu.roll(uh, 2, axis=0))
    first = b + w2 * uh + w1 * u1 + w0 * u2
    del rows
    return full, first


def _ffn_up_kernel(h_ref, wg_ref, wv_ref, cwg_ref, cwv_ref, cbg_ref, cbv_ref,
                   o_ref, carry_ref, *, tiles_per_seq):
    i = pl.program_id(0)
    j = pl.program_id(1)
    tm = h_ref.shape[0]
    head = SUBLANES_BF16
    h = h_ref[...]
    ug = jnp.dot(h, wg_ref[...], preferred_element_type=F32)
    uv = jnp.dot(h, wv_ref[...], preferred_element_type=F32)
    seq_start = (i % tiles_per_seq) == 0
    pg = jnp.where(seq_start, 0.0, carry_ref[j, 0])
    pv = jnp.where(seq_start, 0.0, carry_ref[j, 1])
    carry_ref[j, 0] = ug[tm - SUBLANES_F32:, :]
    carry_ref[j, 1] = uv[tm - SUBLANES_F32:, :]
    g_full, g_first = _causal_conv(ug, pg, cwg_ref, cbg_ref)
    v_full, v_first = _causal_conv(uv, pv, cwv_ref, cbv_ref)
    o_ref[...] = (_gelu_tanh(g_full) * v_full).astype(o_ref.dtype)
    o_ref[0:head, :] = (_gelu_tanh(g_first) * v_first).astype(o_ref.dtype)


def _ffn_up(h2, w_up, conv_w, conv_b, seq, tm=1024, tf=FF_TILE):
    t, k = h2.shape
    n = w_up.shape[1] // 2
    nj = n // tf
    gate = lambda i, j: (0, j)
    value = lambda i, j: (0, nj + j)
    return pl.pallas_call(
        functools.partial(_ffn_up_kernel, tiles_per_seq=seq // tm),
        grid=(t // tm, nj),
        in_specs=[pl.BlockSpec((tm, k), lambda i, j: (i, 0)),
                  pl.BlockSpec((k, tf), gate), pl.BlockSpec((k, tf), value),
                  pl.BlockSpec((CONV_TAPS, tf), gate), pl.BlockSpec((CONV_TAPS, tf), value),
                  pl.BlockSpec((1, tf), gate), pl.BlockSpec((1, tf), value)],
        out_specs=pl.BlockSpec((tm, tf), lambda i, j: (i, j)),
        out_shape=jax.ShapeDtypeStruct((t, n), BF16),
        scratch_shapes=[pltpu.VMEM((nj, 2, SUBLANES_F32, tf), F32)],
        compiler_params=_params(("arbitrary", "arbitrary"), _vmem_estimate(
            [((tm, k), BF16), ((k, tf), BF16), ((k, tf), BF16), ((tm, tf), BF16)],
            scratch=[((nj, 2, SUBLANES_F32, tf), F32)],
            temps=[((tm, tf), F32)] * 6)),
        name="ffn_up",
    )(h2, w_up, w_up, conv_w, conv_w, conv_b, conv_b)


def _pool_kernel(u_ref, halo_ref, pw_ref, ps_ref, o_ref, *, tiles_per_seq):
    i = pl.program_id(0)
    tm = u_ref.shape[0]
    halo = halo_ref.shape[0]
    it = i % tiles_per_seq
    seq_start = it == 0
    pos = (it * tm + 1 + lax.broadcasted_iota(jnp.int32, (tm, 1), 0)).astype(F32)
    gd = POOL_GROUP_DIM
    for g, window in enumerate(POOL_WINDOWS):
        cols = slice(g * gd, (g + 1) * gd)
        u = u_ref[:, cols].astype(F32)
        hl = jnp.where(seq_start, 0.0, halo_ref[:, cols].astype(F32))
        s = jnp.concatenate([hl, u], axis=0)
        shift = 1
        while shift < window:
            s = s + pltpu.roll(s, shift, axis=0)
            shift *= 2
        count = jnp.minimum(pos, float(window))
        pooled = s[halo:, :] / count - u
        mixed = jnp.dot(pooled.astype(BF16), pw_ref[g], preferred_element_type=F32)
        o_ref[:, cols] = (mixed * ps_ref[:, cols]).astype(o_ref.dtype)


def _pool(proj, pool_w, pool_scale, seq, tm=512):
    t = proj.shape[0]
    halo = SUBLANES_BF16
    assert max(POOL_WINDOWS) <= halo
    col_block = (proj.shape[1] - POOL_WIDTH) // POOL_WIDTH
    r = tm // halo
    ng = len(POOL_WINDOWS)
    return pl.pallas_call(
        functools.partial(_pool_kernel, tiles_per_seq=seq // tm),
        grid=(t // tm,),
        in_specs=[pl.BlockSpec((tm, POOL_WIDTH), lambda i: (i, col_block)),
                  pl.BlockSpec((halo, POOL_WIDTH),
                               lambda i: (jnp.maximum(i * r - 1, 0), col_block)),
                  pl.BlockSpec((ng, POOL_GROUP_DIM, POOL_GROUP_DIM), lambda i: (0, 0, 0)),
                  pl.BlockSpec((1, POOL_WIDTH), lambda i: (0, 0))],
        out_specs=pl.BlockSpec((tm, POOL_WIDTH), lambda i: (i, 0)),
        out_shape=jax.ShapeDtypeStruct((t, POOL_WIDTH), BF16),
        compiler_params=_params(("arbitrary",), _vmem_estimate(
            [((tm, POOL_WIDTH), BF16), ((halo, POOL_WIDTH), BF16),
             ((ng, POOL_GROUP_DIM, POOL_GROUP_DIM), BF16), ((tm, POOL_WIDTH), BF16)],
            temps=[((tm + halo, POOL_GROUP_DIM), F32)] * 6)),
        name="pool_mixer",
    )(proj, proj, pool_w, pool_scale)


def _attn_kernel(q_ref, k_ref, v_ref, o_ref, vt_ref, acc_ref, *, tk, unroll, heads):
    qi = pl.program_id(1)
    tq = q_ref.shape[0]
    grp = tk * unroll
    assert grp == tq
    ngrp = k_ref.shape[0] // grp
    qscale = HEAD_DIM ** -0.5 * LOG2E

    def lanes(h):
        return slice(h * HEAD_DIM, (h + 1) * HEAD_DIM)

    @pl.when(qi == 0)
    def _():
        def xpose(g, _):
            rows = pl.ds(pl.multiple_of(g * grp, grp), grp)
            for h in range(heads):
                vt_ref[h, g] = v_ref[rows, lanes(h)].astype(F32).T.astype(BF16)
            return 0
        lax.fori_loop(0, ngrp, xpose, 0)

    qs = [(q_ref[:, lanes(h)].astype(F32) * qscale).astype(BF16) for h in range(heads)]
    rr = lax.broadcasted_iota(jnp.int32, (tk, 2 * tk), 0)
    cc = lax.broadcasted_iota(jnp.int32, (tk, 2 * tk), 1)
    tri2 = jnp.where((cc % tk) >= rr, 1.0, 0.0).astype(BF16)

    acc_ref[...] = jnp.zeros_like(acc_ref)

    def step(g, cs, masked):
        hs = range(heads)
        rows_g = pl.ds(pl.multiple_of(g * grp, grp), grp)
        zs = [lax.dot_general(k_ref[rows_g, lanes(h)], qs[h], (((1,), (1,)), ((), ())),
                              preferred_element_type=F32) for h in hs]
        if masked:
            valid = (lax.broadcasted_iota(jnp.int32, (grp, tq), 0)
                     < lax.broadcasted_iota(jnp.int32, (grp, tq), 1))
        hls = []
        for h in hs:
            z = zs[h]
            nabs = lax.bitcast_convert_type(
                lax.bitcast_convert_type(z, jnp.int32) | jnp.int32(-2 ** 31), F32)
            p = jnp.maximum(z, 0.0) + jnp.log(1.0 + jnp.exp2(nabs)) * LOG2E
            if masked:
                p = jnp.where(valid, p, 0.0)
            hi = p.astype(BF16)
            lo = (p - hi.astype(F32)).astype(BF16)
            hls.append((hi, lo))
        sts = [[None] * unroll for _ in hs]
        for u in reversed(range(unroll)):
            rows = slice(u * tk, (u + 1) * tk)
            for h in hs:
                hi, lo = hls[h]
                sts[h][u] = jnp.dot(tri2, jnp.concatenate([hi[rows], lo[rows]], axis=0),
                                    preferred_element_type=F32)
        wss = []
        cs = list(cs)
        for h in hs:
            ws = [None] * unroll
            for u in reversed(range(unroll)):
                rows = slice(u * tk, (u + 1) * tk)
                w = jnp.exp2(zs[h][rows] - sts[h][u] - cs[h])
                if masked:
                    w = jnp.where(valid[rows], w, 0.0)
                ws[u] = w.astype(BF16)
                cs[h] = cs[h] + sts[h][u][0:1, :]
            wss.append(jnp.concatenate(ws, axis=0))
        for h in hs:
            acc_ref[h] += jnp.dot(vt_ref[h, g], wss[h],
                                  preferred_element_type=F32)
        return tuple(cs)

    cs = step(qi, (jnp.zeros((1, tq), F32),) * heads, True)
    lax.fori_loop(0, qi, lambda it, cs_: step(qi - 1 - it, cs_, False), cs)
    for h in range(heads):
        o_ref[:, lanes(h)] = acc_ref[h].T.astype(o_ref.dtype)


def _attention(proj, batch, seq, tk=ATTN_TK, unroll=ATTN_UNROLL, heads=ATTN_HEADS_PER_STEP):
    t = proj.shape[0]
    tq = tk * unroll
    nq = seq // tq
    hgroups = HEADS // heads
    width = heads * HEAD_DIM

    def bh(g):
        return g // hgroups, g % hgroups

    def q_map(g, qi):
        b, h = bh(g)
        return b * nq + qi, h

    def k_map(g, qi):
        b, h = bh(g)
        return b, hgroups + h

    def v_map(g, qi):
        b, h = bh(g)
        return b, 2 * hgroups + h

    return pl.pallas_call(
        functools.partial(_attn_kernel, tk=tk, unroll=unroll, heads=heads),
        grid=(batch * hgroups, nq),
        in_specs=[pl.BlockSpec((tq, width), q_map),
                  pl.BlockSpec((seq, width), k_map, pipeline_mode=pl.Buffered(1)),
                  pl.BlockSpec((seq, width), v_map, pipeline_mode=pl.Buffered(1))],
        out_specs=pl.BlockSpec((tq, width), q_map),
        out_shape=jax.ShapeDtypeStruct((t, SB_WIDTH), BF16),
        scratch_shapes=[pltpu.VMEM((heads, nq, HEAD_DIM, tq), BF16),
                        pltpu.VMEM((heads, HEAD_DIM, tq), F32)],
        compiler_params=_params(("arbitrary", "arbitrary"), _vmem_estimate(
            [((tq, width), BF16), ((tq, width), BF16)],
            scratch=[((seq, width), BF16)] * 3 + [((width, tq), F32)],
            temps=[((tq, tq), F32)] * 4 * heads)),
        name="stickbreak_attn",
    )(proj, proj, proj)


def _pad_halves(a, dtype):
    rows = a.shape[0]
    a3 = a.astype(dtype).reshape(rows, 2, D_FF)
    return jnp.pad(a3, ((0, 0), (0, 0), (0, D_FF_PAD - D_FF))).reshape(rows, 2 * D_FF_PAD)


def _layer(x, pre_mix_norm, w_in, pool_w, pool_scale, w_out, post_mix_norm,
           pre_ffn_norm, w_up, conv_w, conv_b, w_down, post_ffn_norm, batch, seq):
    vec = lambda g: g.reshape(1, -1).astype(F32)
    w_in_b = w_in.astype(BF16)
    w_out_b = w_out.astype(BF16)
    pool_w_b = pool_w.astype(BF16)
    w_up_b = _pad_halves(w_up, BF16)
    conv_w_p = _pad_halves(conv_w, F32)
    conv_b_p = _pad_halves(conv_b[None, :], F32)
    w_down_b = jnp.pad(w_down.astype(BF16), ((0, D_FF_PAD - D_FF), (0, 0)))

    h = _prenorm(x, vec(pre_mix_norm))
    proj = _in_proj(h, w_in_b)
    attn = _attention(proj, batch, seq)
    pooled = _pool(proj, pool_w_b, vec(pool_scale), seq)
    mix = _out_proj(attn, pooled, w_out_b)
    x1, h2 = _post_mix(x, mix, vec(post_mix_norm), vec(pre_ffn_norm))
    act = _ffn_up(h2, w_up_b, conv_w_p, conv_b_p, seq)
    ffn = _down_proj(act, w_down_b)
    return _post_ffn(x1, ffn, vec(post_ffn_norm))


def kernel(x, pre_mix_norm, w_in, pool_w, pool_scale, w_out, post_mix_norm,
           pre_ffn_norm, w_up, conv_w, conv_b, w_down, post_ffn_norm):
    batch, seq, d = x.shape
    assert d == D_MODEL and w_up.shape[-1] == 2 * D_FF
    xt = x.reshape(batch * seq, d)
    for l in range(pre_mix_norm.shape[0]):
        xt = _layer(xt, pre_mix_norm[l], w_in[l], pool_w[l], pool_scale[l], w_out[l],
                    post_mix_norm[l], pre_ffn_norm[l], w_up[l], conv_w[l], conv_b[l],
                    w_down[l], post_ffn_norm[l], batch, seq)
    return xt.reshape(batch, seq, d)
```

```python
import functools
import math

import jax
import jax.numpy as jnp
from jax import lax
from jax.experimental import pallas as pl
from jax.experimental.pallas import tpu as pltpu

F32 = jnp.float32
BF16 = jnp.bfloat16

D_MODEL = 4096
SB_WIDTH = 2048
HEAD_DIM = 128
HEADS = SB_WIDTH // HEAD_DIM
POOL_WIDTH = 2048
POOL_WINDOWS = (2, 4, 8, 16)
POOL_GROUP_DIM = POOL_WIDTH // len(POOL_WINDOWS)
D_FF = 11008
CONV_TAPS = 3
NORM_EPS = 1e-6

LANES = 128
SUBLANES_F32 = 8
SUBLANES_BF16 = 16
MIB = 1024 * 1024
COMPILER_SCRATCH_BYTES = 6 * MIB

FF_TILE = 512
FF_SUBTILE = 512
D_FF_PAD = ((D_FF + 1023) // 1024) * 1024
DOWN_TK = 1024

ATTN_TK = 128
ATTN_UNROLL = 2
ATTN_HEADS_PER_STEP = 4
ATTN_SKIP_LOG2 = 160.0
LOG2E = math.log2(math.e)


def _params(semantics, vmem_bytes):
    return pltpu.CompilerParams(dimension_semantics=semantics,
                                vmem_limit_bytes=int(vmem_bytes))


def _nbytes(shape, dtype):
    return math.prod(shape) * jnp.dtype(dtype).itemsize


def _vmem_estimate(pipelined, scratch=(), temps=()):
    total = 2 * sum(_nbytes(s, d) for s, d in pipelined)
    total += sum(_nbytes(s, d) for s, d in scratch)
    total += sum(_nbytes(s, d) for s, d in temps)
    return total + COMPILER_SCRATCH_BYTES


def _rms(x, gain):
    return x * lax.rsqrt(jnp.mean(x * x, axis=-1, keepdims=True) + NORM_EPS) * gain


def _prenorm_kernel(x_ref, g_ref, o_ref):
    o_ref[...] = _rms(x_ref[...], g_ref[...]).astype(o_ref.dtype)


def _prenorm(x, gain, tm=256):
    t, d = x.shape
    return pl.pallas_call(
        _prenorm_kernel,
        grid=(t // tm,),
        in_specs=[pl.BlockSpec((tm, d), lambda i: (i, 0)),
                  pl.BlockSpec((1, d), lambda i: (0, 0))],
        out_specs=pl.BlockSpec((tm, d), lambda i: (i, 0)),
        out_shape=jax.ShapeDtypeStruct((t, d), BF16),
        compiler_params=_params(("arbitrary",), _vmem_estimate(
            [((tm, d), F32), ((tm, d), BF16)], temps=[((tm, d), F32)])),
        name="prenorm",
    )(x, gain)


def _post_mix_kernel(x_ref, m_ref, g1_ref, g2_ref, x1_ref, h2_ref):
    x1 = x_ref[...] + _rms(m_ref[...], g1_ref[...])
    x1_ref[...] = x1
    h2_ref[...] = _rms(x1, g2_ref[...]).astype(h2_ref.dtype)


def _post_mix(x, mix, g1, g2, tm=256):
    t, d = x.shape
    row = pl.BlockSpec((tm, d), lambda i: (i, 0))
    vec = pl.BlockSpec((1, d), lambda i: (0, 0))
    return pl.pallas_call(
        _post_mix_kernel,
        grid=(t // tm,),
        in_specs=[row, row, vec, vec],
        out_specs=[row, row],
        out_shape=[jax.ShapeDtypeStruct((t, d), F32), jax.ShapeDtypeStruct((t, d), BF16)],
        compiler_params=_params(("arbitrary",), _vmem_estimate(
            [((tm, d), F32)] * 3 + [((tm, d), BF16)], temps=[((tm, d), F32)] * 2)),
        name="post_mix",
    )(x, mix, g1, g2)


def _post_ffn_kernel(x_ref, f_ref, g_ref, o_ref):
    o_ref[...] = x_ref[...] + _rms(f_ref[...], g_ref[...])


def _post_ffn(x1, ffn, g, tm=256):
    t, d = x1.shape
    row = pl.BlockSpec((tm, d), lambda i: (i, 0))
    vec = pl.BlockSpec((1, d), lambda i: (0, 0))
    return pl.pallas_call(
        _post_ffn_kernel,
        grid=(t // tm,),
        in_specs=[row, row, vec],
        out_specs=row,
        out_shape=jax.ShapeDtypeStruct((t, d), F32),
        compiler_params=_params(("arbitrary",), _vmem_estimate(
            [((tm, d), F32)] * 3, temps=[((tm, d), F32)])),
        name="post_ffn",
    )(x1, ffn, g)


def _in_proj_kernel(a_ref, w_ref, o_ref):
    o_ref[...] = jnp.dot(a_ref[...], w_ref[...],
                         preferred_element_type=F32).astype(o_ref.dtype)


def _in_proj(h, w, tm=1024, tn=1024):
    t, k = h.shape
    n = w.shape[1]
    return pl.pallas_call(
        _in_proj_kernel,
        grid=(t // tm, n // tn),
        in_specs=[pl.BlockSpec((tm, k), lambda i, j: (i, 0)),
                  pl.BlockSpec((k, tn), lambda i, j: (0, j))],
        out_specs=pl.BlockSpec((tm, tn), lambda i, j: (i, j)),
        out_shape=jax.ShapeDtypeStruct((t, n), BF16),
        compiler_params=_params(("arbitrary", "arbitrary"), _vmem_estimate(
            [((tm, k), BF16), ((k, tn), BF16), ((tm, tn), BF16)],
            temps=[((tm, tn), F32)])),
        name="in_proj",
    )(h, w)


def _out_proj_kernel(a_ref, p_ref, wa_ref, wp_ref, o_ref):
    acc = jnp.dot(a_ref[...], wa_ref[...], preferred_element_type=F32)
    acc += jnp.dot(p_ref[...], wp_ref[...], preferred_element_type=F32)
    o_ref[...] = acc


def _out_proj(attn, pooled, w, tm=1024, tn=1024):
    t, ka = attn.shape
    kp = pooled.shape[1]
    assert ka == kp
    n = w.shape[1]
    return pl.pallas_call(
        _out_proj_kernel,
        grid=(t // tm, n // tn),
        in_specs=[pl.BlockSpec((tm, ka), lambda i, j: (i, 0)),
                  pl.BlockSpec((tm, kp), lambda i, j: (i, 0)),
                  pl.BlockSpec((ka, tn), lambda i, j: (0, j)),
                  pl.BlockSpec((kp, tn), lambda i, j: (1, j))],
        out_specs=pl.BlockSpec((tm, tn), lambda i, j: (i, j)),
        out_shape=jax.ShapeDtypeStruct((t, n), F32),
        compiler_params=_params(("arbitrary", "arbitrary"), _vmem_estimate(
            [((tm, ka), BF16), ((tm, kp), BF16), ((ka, tn), BF16), ((kp, tn), BF16),
             ((tm, tn), F32)], temps=[((tm, tn), F32)])),
        name="out_proj",
    )(attn, pooled, w, w)


def _down_proj_kernel(a_ref, w_ref, o_ref):
    @pl.when(pl.program_id(2) == 0)
    def _():
        o_ref[...] = jnp.zeros_like(o_ref)

    o_ref[...] += jnp.dot(a_ref[...], w_ref[...], preferred_element_type=F32)


def _down_proj(act, w, tm=1024, tn=2048, tk=DOWN_TK):
    t, k = act.shape
    n = w.shape[1]
    return pl.pallas_call(
        _down_proj_kernel,
        grid=(t // tm, n // tn, k // tk),
        in_specs=[pl.BlockSpec((tm, tk), lambda i, j, kk: (i, kk)),
                  pl.BlockSpec((tk, tn), lambda i, j, kk: (kk, j))],
        out_specs=pl.BlockSpec((tm, tn), lambda i, j, kk: (i, j)),
        out_shape=jax.ShapeDtypeStruct((t, n), F32),
        compiler_params=_params(("arbitrary",) * 3, _vmem_estimate(
            [((tm, tk), BF16), ((tk, tn), BF16), ((tm, tn), F32)],
            temps=[((tm, tn), F32)])),
        name="down_proj",
    )(act, w)


def _gelu_tanh(x):
    c = math.sqrt(2.0 / math.pi)
    return 0.5 * x * (1.0 + jnp.tanh(c * (x + 0.044715 * (x * x * x))))


def _causal_conv(u, prev, cw, b):
    head = SUBLANES_BF16
    w0 = cw[0:1, :]
    w1 = cw[1:2, :]
    w2 = cw[2:3, :]
    full = b + w2 * u + w1 * pltpu.roll(u, 1, axis=0) + w0 * pltpu.roll(u, 2, axis=0)
    uh = u[:head]
    ph = jnp.concatenate([prev, prev], axis=0)
    r = lax.broadcasted_iota(jnp.int32, uh.shape, 0)
    u1 = jnp.where(r < 1, pltpu.roll(ph, 1, axis=0), pltpu.roll(uh, 1, axis=0))
    u2 = jnp.where(r < 2, pltpu.roll(ph, 2, axis=0), pltpu.roll(uh, 2, axis=0))
    first = b + w2 * uh + w1 * u1 + w0 * u2
    return full, first


def _ffn_up_kernel(h_ref, wg_ref, wv_ref, cwg_ref, cwv_ref, cbg_ref, cbv_ref,
                   o_ref, carry_ref, *, tiles_per_seq, sub):
    i = pl.program_id(0)
    j = pl.program_id(1)
    tm = h_ref.shape[0]
    tf = o_ref.shape[1]
    head = SUBLANES_BF16
    h = h_ref[...]
    seq_start = (i % tiles_per_seq) == 0
    for s in range(tf // sub):
        cols = slice(s * sub, (s + 1) * sub)
        ug = jnp.dot(h, wg_ref[:, cols], preferred_element_type=F32)
        uv = jnp.dot(h, wv_ref[:, cols], preferred_element_type=F32)
        pg = jnp.where(seq_start, 0.0, carry_ref[j, 0, :, cols])
        pv = jnp.where(seq_start, 0.0, carry_ref[j, 1, :, cols])
        carry_ref[j, 0, :, cols] = ug[tm - SUBLANES_F32:, :]
        carry_ref[j, 1, :, cols] = uv[tm - SUBLANES_F32:, :]
        g_full, g_first = _causal_conv(ug, pg, cwg_ref[:, cols], cbg_ref[:, cols])
        v_full, v_first = _causal_conv(uv, pv, cwv_ref[:, cols], cbv_ref[:, cols])
        o_ref[:, cols] = (_gelu_tanh(g_full) * v_full).astype(o_ref.dtype)
        o_ref[0:head, cols] = (_gelu_tanh(g_first) * v_first).astype(o_ref.dtype)


def _ffn_up(h2, w_up, conv_w, conv_b, seq, tm=1024, tf=FF_TILE):
    t, k = h2.shape
    n = w_up.shape[1] // 2
    nj = n // tf
    gate = lambda i, j: (0, j)
    value = lambda i, j: (0, nj + j)
    return pl.pallas_call(
        functools.partial(_ffn_up_kernel, tiles_per_seq=seq // tm, sub=FF_SUBTILE),
        grid=(t // tm, nj),
        in_specs=[pl.BlockSpec((tm, k), lambda i, j: (i, 0)),
                  pl.BlockSpec((k, tf), gate), pl.BlockSpec((k, tf), value),
                  pl.BlockSpec((CONV_TAPS, tf), gate), pl.BlockSpec((CONV_TAPS, tf), value),
                  pl.BlockSpec((1, tf), gate), pl.BlockSpec((1, tf), value)],
        out_specs=pl.BlockSpec((tm, tf), lambda i, j: (i, j)),
        out_shape=jax.ShapeDtypeStruct((t, n), BF16),
        scratch_shapes=[pltpu.VMEM((nj, 2, SUBLANES_F32, tf), F32)],
        compiler_params=_params(("arbitrary", "arbitrary"), _vmem_estimate(
            [((tm, k), BF16), ((k, tf), BF16), ((k, tf), BF16), ((tm, tf), BF16)],
            scratch=[((nj, 2, SUBLANES_F32, tf), F32)],
            temps=[((tm, tf), F32)] * 6)),
        name="ffn_up",
    )(h2, w_up, w_up, conv_w, conv_w, conv_b, conv_b)


def _pool_kernel(u_ref, halo_ref, pw_ref, ps_ref, o_ref, *, tiles_per_seq):
    i = pl.program_id(0)
    tm = u_ref.shape[0]
    halo = halo_ref.shape[0]
    it = i % tiles_per_seq
    seq_start = it == 0
    pos = (it * tm + 1 + lax.broadcasted_iota(jnp.int32, (tm, 1), 0)).astype(F32)
    gd = POOL_GROUP_DIM
    for g, window in enumerate(POOL_WINDOWS):
        cols = slice(g * gd, (g + 1) * gd)
        u = u_ref[:, cols].astype(F32)
        hl = jnp.where(seq_start, 0.0, halo_ref[:, cols].astype(F32))
        s = jnp.concatenate([hl, u], axis=0)
        shift = 1
        while shift < window:
            s = s + pltpu.roll(s, shift, axis=0)
            shift *= 2
        count = jnp.minimum(pos, float(window))
        pooled = s[halo:, :] / count - u
        mixed = jnp.dot(pooled.astype(BF16), pw_ref[g], preferred_element_type=F32)
        o_ref[:, cols] = (mixed * ps_ref[:, cols]).astype(o_ref.dtype)


def _pool(proj, pool_w, pool_scale, seq, tm=512):
    t = proj.shape[0]
    halo = SUBLANES_BF16
    assert max(POOL_WINDOWS) <= halo
    col_block = (proj.shape[1] - POOL_WIDTH) // POOL_WIDTH
    r = tm // halo
    ng = len(POOL_WINDOWS)
    return pl.pallas_call(
        functools.partial(_pool_kernel, tiles_per_seq=seq // tm),
        grid=(t // tm,),
        in_specs=[pl.BlockSpec((tm, POOL_WIDTH), lambda i: (i, col_block)),
                  pl.BlockSpec((halo, POOL_WIDTH),
                               lambda i: (jnp.maximum(i * r - 1, 0), col_block)),
                  pl.BlockSpec((ng, POOL_GROUP_DIM, POOL_GROUP_DIM), lambda i: (0, 0, 0)),
                  pl.BlockSpec((1, POOL_WIDTH), lambda i: (0, 0))],
        out_specs=pl.BlockSpec((tm, POOL_WIDTH), lambda i: (i, 0)),
        out_shape=jax.ShapeDtypeStruct((t, POOL_WIDTH), BF16),
        compiler_params=_params(("arbitrary",), _vmem_estimate(
            [((tm, POOL_WIDTH), BF16), ((halo, POOL_WIDTH), BF16),
             ((ng, POOL_GROUP_DIM, POOL_GROUP_DIM), BF16), ((tm, POOL_WIDTH), BF16)],
            temps=[((tm + halo, POOL_GROUP_DIM), F32)] * 6)),
        name="pool_mixer",
    )(proj, proj, pool_w, pool_scale)


def _attn_kernel(q_ref, k_ref, v_ref, o_ref, vt_ref, acc_ref, *, tk, unroll, heads):
    qi = pl.program_id(1)
    tq = q_ref.shape[0]
    grp = tk * unroll
    assert grp == tq
    ngrp = k_ref.shape[0] // grp
    qscale = HEAD_DIM ** -0.5 * LOG2E

    def lanes(h):
        return slice(h * HEAD_DIM, (h + 1) * HEAD_DIM)

    @pl.when(qi == 0)
    def _():
        def xpose(g, _):
            rows = pl.ds(pl.multiple_of(g * grp, grp), grp)
            for h in range(heads):
                vt_ref[h, g] = v_ref[rows, lanes(h)].astype(F32).T.astype(BF16)
            return 0
        lax.fori_loop(0, ngrp, xpose, 0)

    qs = [(q_ref[:, lanes(h)].astype(F32) * qscale).astype(BF16) for h in range(heads)]
    rr = lax.broadcasted_iota(jnp.int32, (tk, 2 * tk), 0)
    cc = lax.broadcasted_iota(jnp.int32, (tk, 2 * tk), 1)
    tri2 = jnp.where((cc % tk) >= rr, 1.0, 0.0).astype(BF16)

    acc_ref[...] = jnp.zeros_like(acc_ref)

    def step(g, cs, masked):
        hs = range(heads)
        rows_g = pl.ds(pl.multiple_of(g * grp, grp), grp)
        zs = [lax.dot_general(k_ref[rows_g, lanes(h)], qs[h], (((1,), (1,)), ((), ())),
                              preferred_element_type=F32) for h in hs]
        if masked:
            valid = (lax.broadcasted_iota(jnp.int32, (grp, tq), 0)
                     < lax.broadcasted_iota(jnp.int32, (grp, tq), 1))
        hls = []
        for h in hs:
            z = zs[h]
            nabs = lax.bitcast_convert_type(
                lax.bitcast_convert_type(z, jnp.int32) | jnp.int32(-2 ** 31), F32)
            p = jnp.maximum(z, 0.0) + jnp.log(1.0 + jnp.exp2(nabs)) * LOG2E
            if masked:
                p = jnp.where(valid, p, 0.0)
            hi = p.astype(BF16)
            lo = (p - hi.astype(F32)).astype(BF16)
            hls.append((hi, lo))
        sts = [[None] * unroll for _ in hs]
        for u in reversed(range(unroll)):
            rows = slice(u * tk, (u + 1) * tk)
            for h in hs:
                hi, lo = hls[h]
                sts[h][u] = jnp.dot(tri2, jnp.concatenate([hi[rows], lo[rows]], axis=0),
                                    preferred_element_type=F32)
        wss = []
        cs = list(cs)
        for h in hs:
            ws = [None] * unroll
            for u in reversed(range(unroll)):
                rows = slice(u * tk, (u + 1) * tk)
                w = jnp.exp2(zs[h][rows] - sts[h][u] - cs[h])
                if masked:
                    w = jnp.where(valid[rows], w, 0.0)
                ws[u] = w.astype(BF16)
                cs[h] = cs[h] + sts[h][u][0:1, :]
            wss.append(jnp.concatenate(ws, axis=0))
        for h in hs:
            acc_ref[h] += jnp.dot(vt_ref[h, g], wss[h],
                                  preferred_element_type=F32)
        return tuple(cs)

    def smallest(cs):
        return jnp.min(functools.reduce(jnp.minimum, cs))

    def more(state):
        return jnp.logical_and(state[0] < qi, state[1] < ATTN_SKIP_LOG2)

    def older(state):
        it, cs = state[0], step(qi - 1 - state[0], state[2:], False)
        return (it + 1, smallest(cs)) + cs

    cs = step(qi, (jnp.zeros((1, tq), F32),) * heads, True)
    lax.while_loop(more, older, (jnp.int32(0), smallest(cs)) + cs)
    for h in range(heads):
        o_ref[:, lanes(h)] = acc_ref[h].T.astype(o_ref.dtype)


def _attention(proj, batch, seq, tk=ATTN_TK, unroll=ATTN_UNROLL, heads=ATTN_HEADS_PER_STEP):
    t = proj.shape[0]
    tq = tk * unroll
    nq = seq // tq
    hgroups = HEADS // heads
    width = heads * HEAD_DIM

    def bh(g):
        return g // hgroups, g % hgroups

    def q_map(g, qi):
        b, h = bh(g)
        return b * nq + qi, h

    def k_map(g, qi):
        b, h = bh(g)
        return b, hgroups + h

    def v_map(g, qi):
        b, h = bh(g)
        return b, 2 * hgroups + h

    return pl.pallas_call(
        functools.partial(_attn_kernel, tk=tk, unroll=unroll, heads=heads),
        grid=(batch * hgroups, nq),
        in_specs=[pl.BlockSpec((tq, width), q_map),
                  pl.BlockSpec((seq, width), k_map, pipeline_mode=pl.Buffered(1)),
                  pl.BlockSpec((seq, width), v_map, pipeline_mode=pl.Buffered(1))],
        out_specs=pl.BlockSpec((tq, width), q_map),
        out_shape=jax.ShapeDtypeStruct((t, SB_WIDTH), BF16),
        scratch_shapes=[pltpu.VMEM((heads, nq, HEAD_DIM, tq), BF16),
                        pltpu.VMEM((heads, HEAD_DIM, tq), F32)],
        compiler_params=_params(("arbitrary", "arbitrary"), _vmem_estimate(
            [((tq, width), BF16), ((tq, width), BF16)],
            scratch=[((seq, width), BF16)] * 3 + [((width, tq), F32)],
            temps=[((tq, tq), F32)] * 4 * heads)),
        name="stickbreak_attn",
    )(proj, proj, proj)


def _cast_pad_kernel(x_ref, o_ref, *, tiles_in, tiles_out):
    tile = pl.program_id(0) % tiles_out
    o_ref[...] = jnp.where(tile < tiles_in, x_ref[...].astype(o_ref.dtype),
                           jnp.zeros((), o_ref.dtype))


def _cast_pad(a, dtype, axis):
    tile = math.gcd(D_FF, D_FF_PAD)
    tiles_in, tiles_out = D_FF // tile, D_FF_PAD // tile
    sections = a.shape[axis] // D_FF
    other = a.shape[1 - axis]

    def src(c):
        return (c // tiles_out) * tiles_in + jnp.minimum(c % tiles_out, tiles_in - 1)

    if axis == 1:
        block, in_map, out_map = (other, tile), (lambda c: (0, src(c))), (lambda c: (0, c))
        out_shape = (other, sections * D_FF_PAD)
    else:
        block, in_map, out_map = (tile, other), (lambda c: (src(c), 0)), (lambda c: (c, 0))
        out_shape = (sections * D_FF_PAD, other)
    return pl.pallas_call(
        functools.partial(_cast_pad_kernel, tiles_in=tiles_in, tiles_out=tiles_out),
        grid=(sections * tiles_out,),
        in_specs=[pl.BlockSpec(block, in_map)],
        out_specs=pl.BlockSpec(block, out_map),
        out_shape=jax.ShapeDtypeStruct(out_shape, dtype),
        compiler_params=_params(("arbitrary",), _vmem_estimate(
            [(block, a.dtype), (block, dtype)], temps=[(block, a.dtype)])),
        name="cast_pad",
    )(a)


def _pad_halves(a):
    pad = lambda h: jnp.pad(h, ((0, 0), (0, D_FF_PAD - D_FF)))
    return jnp.concatenate([pad(a[:, :D_FF]), pad(a[:, D_FF:])], axis=1)


def _layer(x, pre_mix_norm, w_in, pool_w, pool_scale, w_out, post_mix_norm,
           pre_ffn_norm, w_up, conv_w, conv_b, w_down, post_ffn_norm, batch, seq):
    vec = lambda g: g.reshape(1, -1).astype(F32)
    w_in_b = w_in.astype(BF16)
    w_out_b = w_out.astype(BF16)
    pool_w_b = pool_w.astype(BF16)
    w_up_b = _cast_pad(w_up, BF16, axis=1)
    conv_w_p = _pad_halves(conv_w)
    conv_b_p = _pad_halves(conv_b[None, :])
    w_down_b = _cast_pad(w_down, BF16, axis=0)

    h = _prenorm(x, vec(pre_mix_norm))
    proj = _in_proj(h, w_in_b)
    attn = _attention(proj, batch, seq)
    pooled = _pool(proj, pool_w_b, vec(pool_scale), seq)
    mix = _out_proj(attn, pooled, w_out_b)
    x1, h2 = _post_mix(x, mix, vec(post_mix_norm), vec(pre_ffn_norm))
    act = _ffn_up(h2, w_up_b, conv_w_p, conv_b_p, seq)
    ffn = _down_proj(act, w_down_b)
    return _post_ffn(x1, ffn, vec(post_ffn_norm))


def kernel(x, pre_mix_norm, w_in, pool_w, pool_scale, w_out, post_mix_norm,
           pre_ffn_norm, w_up, conv_w, conv_b, w_down, post_ffn_norm):
    batch, seq, d = x.shape
    assert d == D_MODEL and w_up.shape[-1] == 2 * D_FF
    xt = x.reshape(batch * seq, d)
    for l in range(pre_mix_norm.shape[0]):
        xt = _layer(xt, pre_mix_norm[l], w_in[l], pool_w[l], pool_scale[l], w_out[l],
                    post_mix_norm[l], pre_ffn_norm[l], w_up[l], conv_w[l], conv_b[l],
                    w_down[l], post_ffn_norm[l], batch, seq)
    return xt.reshape(batch, seq, d)
```

```python
import functools
import math

import jax
import jax.numpy as jnp
from jax import lax
from jax.experimental import pallas as pl
from jax.experimental.pallas import tpu as pltpu

F32 = jnp.float32
BF16 = jnp.bfloat16

D_MODEL = 4096
SB_WIDTH = 2048
HEAD_DIM = 128
HEADS = SB_WIDTH // HEAD_DIM
POOL_WIDTH = 2048
POOL_WINDOWS = (2, 4, 8, 16)
POOL_GROUP_DIM = POOL_WIDTH // len(POOL_WINDOWS)
D_FF = 11008
CONV_TAPS = 3
NORM_EPS = 1e-6

LANES = 128
SUBLANES_F32 = 8
SUBLANES_BF16 = 16
MIB = 1024 * 1024
COMPILER_SCRATCH_BYTES = 6 * MIB

FF_TILE = 512
D_FF_PAD = ((D_FF + FF_TILE - 1) // FF_TILE) * FF_TILE

ATTN_TK = 128
ATTN_UNROLL = 2
ATTN_HEADS_PER_STEP = 4
ATTN_SKIP_LOG2 = 160.0
LOG2E = math.log2(math.e)


def _params(semantics, vmem_bytes):
    return pltpu.CompilerParams(dimension_semantics=semantics,
                                vmem_limit_bytes=int(vmem_bytes))


def _nbytes(shape, dtype):
    return math.prod(shape) * jnp.dtype(dtype).itemsize


def _vmem_estimate(pipelined, scratch=(), temps=()):
    total = 2 * sum(_nbytes(s, d) for s, d in pipelined)
    total += sum(_nbytes(s, d) for s, d in scratch)
    total += sum(_nbytes(s, d) for s, d in temps)
    return total + COMPILER_SCRATCH_BYTES


def _rms(x, gain):
    return x * lax.rsqrt(jnp.mean(x * x, axis=-1, keepdims=True) + NORM_EPS) * gain


def _prenorm_kernel(x_ref, g_ref, o_ref):
    o_ref[...] = _rms(x_ref[...], g_ref[...]).astype(o_ref.dtype)


def _prenorm(x, gain, tm=256):
    t, d = x.shape
    return pl.pallas_call(
        _prenorm_kernel,
        grid=(t // tm,),
        in_specs=[pl.BlockSpec((tm, d), lambda i: (i, 0)),
                  pl.BlockSpec((1, d), lambda i: (0, 0))],
        out_specs=pl.BlockSpec((tm, d), lambda i: (i, 0)),
        out_shape=jax.ShapeDtypeStruct((t, d), BF16),
        compiler_params=_params(("arbitrary",), _vmem_estimate(
            [((tm, d), F32), ((tm, d), BF16)], temps=[((tm, d), F32)])),
        name="prenorm",
    )(x, gain)


def _post_mix_kernel(x_ref, m_ref, g1_ref, g2_ref, x1_ref, h2_ref):
    x1 = x_ref[...] + _rms(m_ref[...], g1_ref[...])
    x1_ref[...] = x1
    h2_ref[...] = _rms(x1, g2_ref[...]).astype(h2_ref.dtype)


def _post_mix(x, mix, g1, g2, tm=256):
    t, d = x.shape
    row = pl.BlockSpec((tm, d), lambda i: (i, 0))
    vec = pl.BlockSpec((1, d), lambda i: (0, 0))
    return pl.pallas_call(
        _post_mix_kernel,
        grid=(t // tm,),
        in_specs=[row, row, vec, vec],
        out_specs=[row, row],
        out_shape=[jax.ShapeDtypeStruct((t, d), F32), jax.ShapeDtypeStruct((t, d), BF16)],
        compiler_params=_params(("arbitrary",), _vmem_estimate(
            [((tm, d), F32)] * 3 + [((tm, d), BF16)], temps=[((tm, d), F32)] * 2)),
        name="post_mix",
    )(x, mix, g1, g2)


def _post_ffn_kernel(x_ref, f_ref, g_ref, o_ref):
    o_ref[...] = x_ref[...] + _rms(f_ref[...], g_ref[...])


def _post_ffn(x1, ffn, g, tm=256):
    t, d = x1.shape
    row = pl.BlockSpec((tm, d), lambda i: (i, 0))
    vec = pl.BlockSpec((1, d), lambda i: (0, 0))
    return pl.pallas_call(
        _post_ffn_kernel,
        grid=(t // tm,),
        in_specs=[row, row, vec],
        out_specs=row,
        out_shape=jax.ShapeDtypeStruct((t, d), F32),
        compiler_params=_params(("arbitrary",), _vmem_estimate(
            [((tm, d), F32)] * 3, temps=[((tm, d), F32)])),
        name="post_ffn",
    )(x1, ffn, g)


def _in_proj_kernel(a_ref, w_ref, o_ref):
    o_ref[...] = jnp.dot(a_ref[...], w_ref[...],
                         preferred_element_type=F32).astype(o_ref.dtype)


def _in_proj(h, w, tm=1024, tn=1024):
    t, k = h.shape
    n = w.shape[1]
    return pl.pallas_call(
        _in_proj_kernel,
        grid=(t // tm, n // tn),
        in_specs=[pl.BlockSpec((tm, k), lambda i, j: (i, 0)),
                  pl.BlockSpec((k, tn), lambda i, j: (0, j))],
        out_specs=pl.BlockSpec((tm, tn), lambda i, j: (i, j)),
        out_shape=jax.ShapeDtypeStruct((t, n), BF16),
        compiler_params=_params(("arbitrary", "arbitrary"), _vmem_estimate(
            [((tm, k), BF16), ((k, tn), BF16), ((tm, tn), BF16)],
            temps=[((tm, tn), F32)])),
        name="in_proj",
    )(h, w)


def _out_proj_kernel(a_ref, p_ref, wa_ref, wp_ref, o_ref):
    acc = jnp.dot(a_ref[...], wa_ref[...], preferred_element_type=F32)
    acc += jnp.dot(p_ref[...], wp_ref[...], preferred_element_type=F32)
    o_ref[...] = acc


def _out_proj(attn, pooled, w, tm=1024, tn=1024):
    t, ka = attn.shape
    kp = pooled.shape[1]
    assert ka == kp
    n = w.shape[1]
    return pl.pallas_call(
        _out_proj_kernel,
        grid=(t // tm, n // tn),
        in_specs=[pl.BlockSpec((tm, ka), lambda i, j: (i, 0)),
                  pl.BlockSpec((tm, kp), lambda i, j: (i, 0)),
                  pl.BlockSpec((ka, tn), lambda i, j: (0, j)),
                  pl.BlockSpec((kp, tn), lambda i, j: (1, j))],
        out_specs=pl.BlockSpec((tm, tn), lambda i, j: (i, j)),
        out_shape=jax.ShapeDtypeStruct((t, n), F32),
        compiler_params=_params(("arbitrary", "arbitrary"), _vmem_estimate(
            [((tm, ka), BF16), ((tm, kp), BF16), ((ka, tn), BF16), ((kp, tn), BF16),
             ((tm, tn), F32)], temps=[((tm, tn), F32)])),
        name="out_proj",
    )(attn, pooled, w, w)


def _down_proj_kernel(a_ref, w_ref, o_ref):
    o_ref[...] = jnp.dot(a_ref[...], w_ref[...], preferred_element_type=F32)


def _down_proj(act, w, tm=512, tn=512):
    t, k = act.shape
    n = w.shape[1]
    return pl.pallas_call(
        _down_proj_kernel,
        grid=(t // tm, n // tn),
        in_specs=[pl.BlockSpec((tm, k), lambda i, j: (i, 0)),
                  pl.BlockSpec((k, tn), lambda i, j: (0, j))],
        out_specs=pl.BlockSpec((tm, tn), lambda i, j: (i, j)),
        out_shape=jax.ShapeDtypeStruct((t, n), F32),
        compiler_params=_params(("arbitrary", "arbitrary"), _vmem_estimate(
            [((tm, k), BF16), ((k, tn), BF16), ((tm, tn), F32)],
            temps=[((tm, tn), F32)])),
        name="down_proj",
    )(act, w)


def _gelu_tanh(x):
    c = math.sqrt(2.0 / math.pi)
    return 0.5 * x * (1.0 + jnp.tanh(c * (x + 0.044715 * (x * x * x))))


def _causal_conv(u, prev, cw, b):
    head = SUBLANES_BF16
    w0 = cw[0:1, :]
    w1 = cw[1:2, :]
    w2 = cw[2:3, :]
    full = b + w2 * u + w1 * pltpu.roll(u, 1, axis=0) + w0 * pltpu.roll(u, 2, axis=0)
    uh = u[:head]
    ph = jnp.concatenate([prev, prev], axis=0)
    r = lax.broadcasted_iota(jnp.int32, uh.shape, 0)
    u1 = jnp.where(r < 1, pltpu.roll(ph, 1, axis=0), pltpu.roll(uh, 1, axis=0))
    u2 = jnp.where(r < 2, pltpu.roll(ph, 2, axis=0), pltpu.roll(uh, 2, axis=0))
    first = b + w2 * uh + w1 * u1 + w0 * u2
    return full, first


def _ffn_up_kernel(h_ref, wg_ref, wv_ref, cwg_ref, cwv_ref, cbg_ref, cbv_ref,
                   o_ref, carry_ref, *, tiles_per_seq):
    i = pl.program_id(0)
    j = pl.program_id(1)
    tm = h_ref.shape[0]
    head = SUBLANES_BF16
    h = h_ref[...]
    ug = jnp.dot(h, wg_ref[...], preferred_element_type=F32)
    uv = jnp.dot(h, wv_ref[...], preferred_element_type=F32)
    seq_start = (i % tiles_per_seq) == 0
    pg = jnp.where(seq_start, 0.0, carry_ref[j, 0])
    pv = jnp.where(seq_start, 0.0, carry_ref[j, 1])
    carry_ref[j, 0] = ug[tm - SUBLANES_F32:, :]
    carry_ref[j, 1] = uv[tm - SUBLANES_F32:, :]
    g_full, g_first = _causal_conv(ug, pg, cwg_ref[...], cbg_ref[...])
    v_full, v_first = _causal_conv(uv, pv, cwv_ref[...], cbv_ref[...])
    o_ref[...] = (_gelu_tanh(g_full) * v_full).astype(o_ref.dtype)
    o_ref[0:head, :] = (_gelu_tanh(g_first) * v_first).astype(o_ref.dtype)


def _ffn_up(h2, w_up, conv_w, conv_b, seq, tm=1024, tf=FF_TILE):
    t, k = h2.shape
    n = w_up.shape[1] // 2
    nj = n // tf
    gate = lambda i, j: (0, j)
    value = lambda i, j: (0, nj + j)
    return pl.pallas_call(
        functools.partial(_ffn_up_kernel, tiles_per_seq=seq // tm),
        grid=(t // tm, nj),
        in_specs=[pl.BlockSpec((tm, k), lambda i, j: (i, 0)),
                  pl.BlockSpec((k, tf), gate), pl.BlockSpec((k, tf), value),
                  pl.BlockSpec((CONV_TAPS, tf), gate), pl.BlockSpec((CONV_TAPS, tf), value),
                  pl.BlockSpec((1, tf), gate), pl.BlockSpec((1, tf), value)],
        out_specs=pl.BlockSpec((tm, tf), lambda i, j: (i, j)),
        out_shape=jax.ShapeDtypeStruct((t, n), BF16),
        scratch_shapes=[pltpu.VMEM((nj, 2, SUBLANES_F32, tf), F32)],
        compiler_params=_params(("arbitrary", "arbitrary"), _vmem_estimate(
            [((tm, k), BF16), ((k, tf), BF16), ((k, tf), BF16), ((tm, tf), BF16)],
            scratch=[((nj, 2, SUBLANES_F32, tf), F32)],
            temps=[((tm, tf), F32)] * 6)),
        name="ffn_up",
    )(h2, w_up, w_up, conv_w, conv_w, conv_b, conv_b)


def _pool_kernel(u_ref, halo_ref, pw_ref, ps_ref, o_ref, *, tiles_per_seq):
    i = pl.program_id(0)
    tm = u_ref.shape[0]
    halo = halo_ref.shape[0]
    it = i % tiles_per_seq
    seq_start = it == 0
    pos = (it * tm + 1 + lax.broadcasted_iota(jnp.int32, (tm, 1), 0)).astype(F32)
    gd = POOL_GROUP_DIM
    for g, window in enumerate(POOL_WINDOWS):
        cols = slice(g * gd, (g + 1) * gd)
        u = u_ref[:, cols].astype(F32)
        hl = jnp.where(seq_start, 0.0, halo_ref[:, cols].astype(F32))
        s = jnp.concatenate([hl, u], axis=0)
        shift = 1
        while shift < window:
            s = s + pltpu.roll(s, shift, axis=0)
            shift *= 2
        count = jnp.minimum(pos, float(window))
        pooled = s[halo:, :] / count - u
        mixed = jnp.dot(pooled.astype(BF16), pw_ref[g], preferred_element_type=F32)
        o_ref[:, cols] = (mixed * ps_ref[:, cols]).astype(o_ref.dtype)


def _pool(proj, pool_w, pool_scale, seq, tm=512):
    t = proj.shape[0]
    halo = SUBLANES_BF16
    assert max(POOL_WINDOWS) <= halo
    col_block = (proj.shape[1] - POOL_WIDTH) // POOL_WIDTH
    r = tm // halo
    ng = len(POOL_WINDOWS)
    return pl.pallas_call(
        functools.partial(_pool_kernel, tiles_per_seq=seq // tm),
        grid=(t // tm,),
        in_specs=[pl.BlockSpec((tm, POOL_WIDTH), lambda i: (i, col_block)),
                  pl.BlockSpec((halo, POOL_WIDTH),
                               lambda i: (jnp.maximum(i * r - 1, 0), col_block)),
                  pl.BlockSpec((ng, POOL_GROUP_DIM, POOL_GROUP_DIM), lambda i: (0, 0, 0)),
                  pl.BlockSpec((1, POOL_WIDTH), lambda i: (0, 0))],
        out_specs=pl.BlockSpec((tm, POOL_WIDTH), lambda i: (i, 0)),
        out_shape=jax.ShapeDtypeStruct((t, POOL_WIDTH), BF16),
        compiler_params=_params(("arbitrary",), _vmem_estimate(
            [((tm, POOL_WIDTH), BF16), ((halo, POOL_WIDTH), BF16),
             ((ng, POOL_GROUP_DIM, POOL_GROUP_DIM), BF16), ((tm, POOL_WIDTH), BF16)],
            temps=[((tm + halo, POOL_GROUP_DIM), F32)] * 6)),
        name="pool_mixer",
    )(proj, proj, pool_w, pool_scale)


def _attn_kernel(q_ref, k_ref, v_ref, o_ref, vt_ref, acc_ref, *, tk, unroll, heads):
    qi = pl.program_id(1)
    tq = q_ref.shape[0]
    grp = tk * unroll
    assert grp == tq
    ngrp = k_ref.shape[0] // grp
    qscale = HEAD_DIM ** -0.5 * LOG2E

    def lanes(h):
        return slice(h * HEAD_DIM, (h + 1) * HEAD_DIM)

    @pl.when(qi == 0)
    def _():
        def xpose(g, _):
            rows = pl.ds(pl.multiple_of(g * grp, grp), grp)
            for h in range(heads):
                vt_ref[h, g] = v_ref[rows, lanes(h)].astype(F32).T.astype(BF16)
            return 0
        lax.fori_loop(0, ngrp, xpose, 0)

    qs = [(q_ref[:, lanes(h)].astype(F32) * qscale).astype(BF16) for h in range(heads)]
    rr = lax.broadcasted_iota(jnp.int32, (tk, 2 * tk), 0)
    cc = lax.broadcasted_iota(jnp.int32, (tk, 2 * tk), 1)
    tri2 = jnp.where((cc % tk) >= rr, 1.0, 0.0).astype(BF16)

    def sweep(groups, cs, init=False):
        hs = range(heads)
        valid = None
        if any(masked for _, masked, _ in groups):
            valid = (lax.broadcasted_iota(jnp.int32, (grp, tq), 0)
                     < lax.broadcasted_iota(jnp.int32, (grp, tq), 1))
        zs = []
        for g, _, _ in groups:
            rows_g = pl.ds(pl.multiple_of(g * grp, grp), grp)
            zs.append([lax.dot_general(k_ref[rows_g, lanes(h)], qs[h],
                                       (((1,), (1,)), ((), ())),
                                       preferred_element_type=F32) for h in hs])
        hls = []
        for n, (_, masked, _) in enumerate(groups):
            hls.append([])
            for h in hs:
                z = zs[n][h]
                p = jnp.maximum(z, 0.0) + jnp.log(1.0 + jnp.exp2(-jnp.abs(z))) * LOG2E
                if masked:
                    p = jnp.where(valid, p, 0.0)
                hi = p.astype(BF16)
                lo = (p - hi.astype(F32)).astype(BF16)
                hls[n].append((hi, lo))
        sts = [[[None] * unroll for _ in hs] for _ in groups]
        for n in range(len(groups)):
            for u in reversed(range(unroll)):
                rows = slice(u * tk, (u + 1) * tk)
                for h in hs:
                    hi, lo = hls[n][h]
                    sts[n][h][u] = jnp.dot(
                        tri2, jnp.concatenate([hi[rows], lo[rows]], axis=0),
                        preferred_element_type=F32)
        cs = list(cs)
        wss = []
        for n, (_, masked, live) in enumerate(groups):
            wss.append([])
            for h in hs:
                ws = [None] * unroll
                for u in reversed(range(unroll)):
                    rows = slice(u * tk, (u + 1) * tk)
                    w = jnp.exp2(zs[n][h][rows] - sts[n][h][u] - cs[h])
                    total = sts[n][h][u][0:1, :]
                    if masked:
                        w = jnp.where(valid[rows], w, 0.0)
                    if live is not None:
                        w = jnp.where(live, w, 0.0)
                        total = jnp.where(live, total, 0.0)
                    ws[u] = w.astype(BF16)
                    cs[h] = cs[h] + total
                wss[n].append(jnp.concatenate(ws, axis=0))
        for h in hs:
            vt = jnp.concatenate([vt_ref[h, g] for g, _, _ in groups], axis=1)
            wv = jnp.dot(vt, jnp.concatenate([ws[h] for ws in wss], axis=0),
                         preferred_element_type=F32)
            if init:
                acc_ref[h] = wv
            else:
                acc_ref[h] += wv
        return tuple(cs)

    def smallest(cs):
        return jnp.min(functools.reduce(jnp.minimum, cs))

    def more(state):
        return jnp.logical_and(state[0] < qi, state[1] < ATTN_SKIP_LOG2)

    def older(state):
        it, cs = state[0], sweep([(qi - 1 - state[0], False, None)], state[2:])
        return (it + 1, smallest(cs)) + cs

    cs = sweep([(qi, True, None), (jnp.maximum(qi - 1, 0), False, qi > 0)],
               (jnp.zeros((1, tq), F32),) * heads, init=True)
    lax.while_loop(more, older, (jnp.int32(1), smallest(cs)) + cs)
    for h in range(heads):
        o_ref[:, lanes(h)] = acc_ref[h].T.astype(o_ref.dtype)


def _attention(proj, batch, seq, tk=ATTN_TK, unroll=ATTN_UNROLL, heads=ATTN_HEADS_PER_STEP):
    t = proj.shape[0]
    tq = tk * unroll
    nq = seq // tq
    hgroups = HEADS // heads
    width = heads * HEAD_DIM

    def bh(g):
        return g // hgroups, g % hgroups

    def q_map(g, qi):
        b, h = bh(g)
        return b * nq + qi, h

    def k_map(g, qi):
        b, h = bh(g)
        return b, hgroups + h

    def v_map(g, qi):
        b, h = bh(g)
        return b, 2 * hgroups + h

    return pl.pallas_call(
        functools.partial(_attn_kernel, tk=tk, unroll=unroll, heads=heads),
        grid=(batch * hgroups, nq),
        in_specs=[pl.BlockSpec((tq, width), q_map),
                  pl.BlockSpec((seq, width), k_map, pipeline_mode=pl.Buffered(1)),
                  pl.BlockSpec((seq, width), v_map, pipeline_mode=pl.Buffered(1))],
        out_specs=pl.BlockSpec((tq, width), q_map),
        out_shape=jax.ShapeDtypeStruct((t, SB_WIDTH), BF16),
        scratch_shapes=[pltpu.VMEM((heads, nq, HEAD_DIM, tq), BF16),
                        pltpu.VMEM((heads, HEAD_DIM, tq), F32)],
        compiler_params=_params(("arbitrary", "arbitrary"), _vmem_estimate(
            [((tq, width), BF16), ((tq, width), BF16)],
            scratch=[((seq, width), BF16)] * 3 + [((width, tq), F32)],
            temps=[((tq, tq), F32)] * 4 * heads)),
        name="stickbreak_attn",
    )(proj, proj, proj)


def _cast_pad_kernel(x_ref, o_ref, *, tiles_in, tiles_out):
    tile = pl.program_id(0) % tiles_out
    o_ref[...] = jnp.where(tile < tiles_in, x_ref[...].astype(o_ref.dtype),
                           jnp.zeros((), o_ref.dtype))


def _cast_pad(a, dtype, axis):
    tile = math.gcd(D_FF, D_FF_PAD)
    tiles_in, tiles_out = D_FF // tile, D_FF_PAD // tile
    sections = a.shape[axis] // D_FF
    other = a.shape[1 - axis]

    def src(c):
        return (c // tiles_out) * tiles_in + jnp.minimum(c % tiles_out, tiles_in - 1)

    if axis == 1:
        block, in_map, out_map = (other, tile), (lambda c: (0, src(c))), (lambda c: (0, c))
        out_shape = (other, sections * D_FF_PAD)
    else:
        block, in_map, out_map = (tile, other), (lambda c: (src(c), 0)), (lambda c: (c, 0))
        out_shape = (sections * D_FF_PAD, other)
    return pl.pallas_call(
        functools.partial(_cast_pad_kernel, tiles_in=tiles_in, tiles_out=tiles_out),
        grid=(sections * tiles_out,),
        in_specs=[pl.BlockSpec(block, in_map)],
        out_specs=pl.BlockSpec(block, out_map),
        out_shape=jax.ShapeDtypeStruct(out_shape, dtype),
        compiler_params=_params(("arbitrary",), _vmem_estimate(
            [(block, a.dtype), (block, dtype)], temps=[(block, a.dtype)])),
        name="cast_pad",
    )(a)


def _pad_halves(a):
    pad = lambda h: jnp.pad(h, ((0, 0), (0, D_FF_PAD - D_FF)))
    return jnp.concatenate([pad(a[:, :D_FF]), pad(a[:, D_FF:])], axis=1)


def _layer(x, pre_mix_norm, w_in, pool_w, pool_scale, w_out, post_mix_norm,
           pre_ffn_norm, w_up, conv_w, conv_b, w_down, post_ffn_norm, batch, seq):
    vec = lambda g: g.reshape(1, -1).astype(F32)
    w_in_b = w_in.astype(BF16)
    w_out_b = w_out.astype(BF16)
    pool_w_b = pool_w.astype(BF16)
    w_up_b = _cast_pad(w_up, BF16, axis=1)
    conv_w_p = _pad_halves(conv_w)
    conv_b_p = _pad_halves(conv_b[None, :])
    w_down_b = _cast_pad(w_down, BF16, axis=0)

    h = _prenorm(x, vec(pre_mix_norm))
    proj = _in_proj(h, w_in_b)
    attn = _attention(proj, batch, seq)
    pooled = _pool(proj, pool_w_b, vec(pool_scale), seq)
    mix = _out_proj(attn, pooled, w_out_b)
    x1, h2 = _post_mix(x, mix, vec(post_mix_norm), vec(pre_ffn_norm))
    act = _ffn_up(h2, w_up_b, conv_w_p, conv_b_p, seq)
    ffn = _down_proj(act, w_down_b)
    return _post_ffn(x1, ffn, vec(post_ffn_norm))


def kernel(x, pre_mix_norm, w_in, pool_w, pool_scale, w_out, post_mix_norm,
           pre_ffn_norm, w_up, conv_w, conv_b, w_down, post_ffn_norm):
    batch, seq, d = x.shape
    assert d == D_MODEL and w_up.shape[-1] == 2 * D_FF
    xt = x.reshape(batch * seq, d)
    for l in range(pre_mix_norm.shape[0]):
        xt = _layer(xt, pre_mix_norm[l], w_in[l], pool_w[l], pool_scale[l], w_out[l],
                    post_mix_norm[l], pre_ffn_norm[l], w_up[l], conv_w[l], conv_b[l],
                    w_down[l], post_ffn_norm[l], batch, seq)
    return xt.reshape(batch, seq, d)
```

```python
import functools
import math

import jax
import jax.numpy as jnp
from jax import lax
from jax.experimental import pallas as pl
from jax.experimental.pallas import tpu as pltpu

F32 = jnp.float32
BF16 = jnp.bfloat16

D_MODEL = 4096
SB_WIDTH = 2048
HEAD_DIM = 128
HEADS = SB_WIDTH // HEAD_DIM
POOL_WIDTH = 2048
POOL_WINDOWS = (2, 4, 8, 16)
POOL_GROUP_DIM = POOL_WIDTH // len(POOL_WINDOWS)
D_FF = 11008
CONV_TAPS = 3
NORM_EPS = 1e-6

LANES = 128
SUBLANES_F32 = 8
SUBLANES_BF16 = 16
MIB = 1024 * 1024
COMPILER_SCRATCH_BYTES = 6 * MIB

FF_TILE = 512
D_FF_PAD = ((D_FF + FF_TILE - 1) // FF_TILE) * FF_TILE

ATTN_TK = 128
ATTN_UNROLL = 2
ATTN_HEADS_PER_STEP = 4
ATTN_SKIP_LOG2 = 160.0
LOG2E = math.log2(math.e)


def _params(semantics, vmem_bytes):
    return pltpu.CompilerParams(dimension_semantics=semantics,
                                vmem_limit_bytes=int(vmem_bytes))


def _nbytes(shape, dtype):
    return math.prod(shape) * jnp.dtype(dtype).itemsize


def _vmem_estimate(pipelined, scratch=(), temps=()):
    total = 2 * sum(_nbytes(s, d) for s, d in pipelined)
    total += sum(_nbytes(s, d) for s, d in scratch)
    total += sum(_nbytes(s, d) for s, d in temps)
    return total + COMPILER_SCRATCH_BYTES


def _rms(x, gain):
    return x * lax.rsqrt(jnp.mean(x * x, axis=-1, keepdims=True) + NORM_EPS) * gain


def _prenorm_kernel(x_ref, g_ref, o_ref):
    o_ref[...] = _rms(x_ref[...], g_ref[...]).astype(o_ref.dtype)


def _prenorm(x, gain, tm=256):
    t, d = x.shape
    return pl.pallas_call(
        _prenorm_kernel,
        grid=(t // tm,),
        in_specs=[pl.BlockSpec((tm, d), lambda i: (i, 0)),
                  pl.BlockSpec((1, d), lambda i: (0, 0))],
        out_specs=pl.BlockSpec((tm, d), lambda i: (i, 0)),
        out_shape=jax.ShapeDtypeStruct((t, d), BF16),
        compiler_params=_params(("arbitrary",), _vmem_estimate(
            [((tm, d), F32), ((tm, d), BF16)], temps=[((tm, d), F32)])),
        name="prenorm",
    )(x, gain)


def _post_mix_kernel(x_ref, m_ref, g1_ref, g2_ref, x1_ref, h2_ref):
    x1 = x_ref[...] + _rms(m_ref[...], g1_ref[...])
    x1_ref[...] = x1
    h2_ref[...] = _rms(x1, g2_ref[...]).astype(h2_ref.dtype)


def _post_mix(x, mix, g1, g2, tm=256):
    t, d = x.shape
    row = pl.BlockSpec((tm, d), lambda i: (i, 0))
    vec = pl.BlockSpec((1, d), lambda i: (0, 0))
    return pl.pallas_call(
        _post_mix_kernel,
        grid=(t // tm,),
        in_specs=[row, row, vec, vec],
        out_specs=[row, row],
        out_shape=[jax.ShapeDtypeStruct((t, d), F32), jax.ShapeDtypeStruct((t, d), BF16)],
        compiler_params=_params(("arbitrary",), _vmem_estimate(
            [((tm, d), F32)] * 3 + [((tm, d), BF16)], temps=[((tm, d), F32)] * 2)),
        name="post_mix",
    )(x, mix, g1, g2)


def _post_ffn_kernel(x_ref, f_ref, g_ref, o_ref):
    o_ref[...] = x_ref[...] + _rms(f_ref[...], g_ref[...])


def _post_ffn(x1, ffn, g, tm=256):
    t, d = x1.shape
    row = pl.BlockSpec((tm, d), lambda i: (i, 0))
    vec = pl.BlockSpec((1, d), lambda i: (0, 0))
    return pl.pallas_call(
        _post_ffn_kernel,
        grid=(t // tm,),
        in_specs=[row, row, vec],
        out_specs=row,
        out_shape=jax.ShapeDtypeStruct((t, d), F32),
        compiler_params=_params(("arbitrary",), _vmem_estimate(
            [((tm, d), F32)] * 3, temps=[((tm, d), F32)])),
        name="post_ffn",
    )(x1, ffn, g)


def _in_proj_kernel(a_ref, w_ref, o_ref):
    o_ref[...] = jnp.dot(a_ref[...], w_ref[...],
                         preferred_element_type=F32).astype(o_ref.dtype)


def _in_proj(h, w, tm=1024, tn=1024):
    t, k = h.shape
    n = w.shape[1]
    return pl.pallas_call(
        _in_proj_kernel,
        grid=(t // tm, n // tn),
        in_specs=[pl.BlockSpec((tm, k), lambda i, j: (i, 0)),
                  pl.BlockSpec((k, tn), lambda i, j: (0, j))],
        out_specs=pl.BlockSpec((tm, tn), lambda i, j: (i, j)),
        out_shape=jax.ShapeDtypeStruct((t, n), BF16),
        compiler_params=_params(("arbitrary", "arbitrary"), _vmem_estimate(
            [((tm, k), BF16), ((k, tn), BF16), ((tm, tn), BF16)],
            temps=[((tm, tn), F32)])),
        name="in_proj",
    )(h, w)


def _out_proj_kernel(a_ref, p_ref, wa_ref, wp_ref, o_ref):
    acc = jnp.dot(a_ref[...], wa_ref[...], preferred_element_type=F32)
    acc += jnp.dot(p_ref[...], wp_ref[...], preferred_element_type=F32)
    o_ref[...] = acc


def _out_proj(attn, pooled, w, tm=1024, tn=1024):
    t, ka = attn.shape
    kp = pooled.shape[1]
    assert ka == kp
    n = w.shape[1]
    return pl.pallas_call(
        _out_proj_kernel,
        grid=(t // tm, n // tn),
        in_specs=[pl.BlockSpec((tm, ka), lambda i, j: (i, 0)),
                  pl.BlockSpec((tm, kp), lambda i, j: (i, 0)),
                  pl.BlockSpec((ka, tn), lambda i, j: (0, j)),
                  pl.BlockSpec((kp, tn), lambda i, j: (1, j))],
        out_specs=pl.BlockSpec((tm, tn), lambda i, j: (i, j)),
        out_shape=jax.ShapeDtypeStruct((t, n), F32),
        compiler_params=_params(("arbitrary", "arbitrary"), _vmem_estimate(
            [((tm, ka), BF16), ((tm, kp), BF16), ((ka, tn), BF16), ((kp, tn), BF16),
             ((tm, tn), F32)], temps=[((tm, tn), F32)])),
        name="out_proj",
    )(attn, pooled, w, w)


def _down_proj_kernel(a_ref, w_ref, o_ref):
    o_ref[...] = jnp.dot(a_ref[...], w_ref[...], preferred_element_type=F32)


def _down_proj(act, w, tm=512, tn=512):
    t, k = act.shape
    n = w.shape[1]
    return pl.pallas_call(
        _down_proj_kernel,
        grid=(n // tn, t // tm),
        in_specs=[pl.BlockSpec((tm, k), lambda j, i: (i, 0)),
                  pl.BlockSpec((k, tn), lambda j, i: (0, j))],
        out_specs=pl.BlockSpec((tm, tn), lambda j, i: (i, j)),
        out_shape=jax.ShapeDtypeStruct((t, n), F32),
        compiler_params=_params(("arbitrary", "arbitrary"), _vmem_estimate(
            [((tm, k), BF16), ((k, tn), BF16), ((tm, tn), F32)],
            temps=[((tm, tn), F32)])),
        name="down_proj",
    )(act, w)


def _gelu_tanh(x):
    c = math.sqrt(2.0 / math.pi)
    return 0.5 * x * (1.0 + jnp.tanh(x * (c + (c * 0.044715) * (x * x))))


def _causal_conv(u, prev, cw, b):
    head = SUBLANES_BF16
    w0 = cw[0:1, :]
    w1 = cw[1:2, :]
    w2 = cw[2:3, :]
    full = b + w2 * u + w1 * pltpu.roll(u, 1, axis=0) + w0 * pltpu.roll(u, 2, axis=0)
    uh = u[:head]
    ph = jnp.concatenate([prev, prev], axis=0)
    r = lax.broadcasted_iota(jnp.int32, uh.shape, 0)
    u1 = jnp.where(r < 1, pltpu.roll(ph, 1, axis=0), pltpu.roll(uh, 1, axis=0))
    u2 = jnp.where(r < 2, pltpu.roll(ph, 2, axis=0), pltpu.roll(uh, 2, axis=0))
    first = b + w2 * uh + w1 * u1 + w0 * u2
    return full, first


def _ffn_up_kernel(h_ref, wg_ref, wv_ref, cwg_ref, cwv_ref, cbg_ref, cbv_ref,
                   o_ref, carry_ref, *, tiles_per_seq):
    i = pl.program_id(0)
    j = pl.program_id(1)
    tm = h_ref.shape[0]
    head = SUBLANES_BF16
    h = h_ref[...]
    ug = jnp.dot(h, wg_ref[...], preferred_element_type=F32)
    uv = jnp.dot(h, wv_ref[...], preferred_element_type=F32)
    seq_start = (i % tiles_per_seq) == 0
    pg = jnp.where(seq_start, 0.0, carry_ref[j, 0])
    pv = jnp.where(seq_start, 0.0, carry_ref[j, 1])
    carry_ref[j, 0] = ug[tm - SUBLANES_F32:, :]
    carry_ref[j, 1] = uv[tm - SUBLANES_F32:, :]
    g_full, g_first = _causal_conv(ug, pg, cwg_ref[...], cbg_ref[...])
    v_full, v_first = _causal_conv(uv, pv, cwv_ref[...], cbv_ref[...])
    o_ref[...] = (_gelu_tanh(g_full) * v_full).astype(o_ref.dtype)
    o_ref[0:head, :] = (_gelu_tanh(g_first) * v_first).astype(o_ref.dtype)


def _ffn_up(h2, w_up, conv_w, conv_b, seq, tm=1024, tf=FF_TILE):
    t, k = h2.shape
    n = w_up.shape[1] // 2
    nj = n // tf
    gate = lambda i, j: (0, j)
    value = lambda i, j: (0, nj + j)
    return pl.pallas_call(
        functools.partial(_ffn_up_kernel, tiles_per_seq=seq // tm),
        grid=(t // tm, nj),
        in_specs=[pl.BlockSpec((tm, k), lambda i, j: (i, 0)),
                  pl.BlockSpec((k, tf), gate), pl.BlockSpec((k, tf), value),
                  pl.BlockSpec((CONV_TAPS, tf), gate), pl.BlockSpec((CONV_TAPS, tf), value),
                  pl.BlockSpec((1, tf), gate), pl.BlockSpec((1, tf), value)],
        out_specs=pl.BlockSpec((tm, tf), lambda i, j: (i, j)),
        out_shape=jax.ShapeDtypeStruct((t, n), BF16),
        scratch_shapes=[pltpu.VMEM((nj, 2, SUBLANES_F32, tf), F32)],
        compiler_params=_params(("arbitrary", "arbitrary"), _vmem_estimate(
            [((tm, k), BF16), ((k, tf), BF16), ((k, tf), BF16), ((tm, tf), BF16)],
            scratch=[((nj, 2, SUBLANES_F32, tf), F32)],
            temps=[((tm, tf), F32)] * 6)),
        name="ffn_up",
    )(h2, w_up, w_up, conv_w, conv_w, conv_b, conv_b)


def _pool_kernel(u_ref, halo_ref, pw_ref, ps_ref, o_ref, *, tiles_per_seq):
    i = pl.program_id(0)
    tm = u_ref.shape[0]
    halo = halo_ref.shape[0]
    it = i % tiles_per_seq
    seq_start = it == 0
    pos = (it * tm + 1 + lax.broadcasted_iota(jnp.int32, (tm, 1), 0)).astype(F32)
    gd = POOL_GROUP_DIM
    for g, window in enumerate(POOL_WINDOWS):
        cols = slice(g * gd, (g + 1) * gd)
        u = u_ref[:, cols].astype(F32)
        hl = jnp.where(seq_start, 0.0, halo_ref[:, cols].astype(F32))
        s = jnp.concatenate([hl, u], axis=0)
        shift = 1
        while shift < window:
            s = s + pltpu.roll(s, shift, axis=0)
            shift *= 2
        count = jnp.minimum(pos, float(window))
        pooled = s[halo:, :] / count - u
        mixed = jnp.dot(pooled.astype(BF16), pw_ref[g], preferred_element_type=F32)
        o_ref[:, cols] = (mixed * ps_ref[:, cols]).astype(o_ref.dtype)


def _pool(proj, pool_w, pool_scale, seq, tm=512):
    t = proj.shape[0]
    halo = SUBLANES_BF16
    assert max(POOL_WINDOWS) <= halo
    col_block = (proj.shape[1] - POOL_WIDTH) // POOL_WIDTH
    r = tm // halo
    ng = len(POOL_WINDOWS)
    return pl.pallas_call(
        functools.partial(_pool_kernel, tiles_per_seq=seq // tm),
        grid=(t // tm,),
        in_specs=[pl.BlockSpec((tm, POOL_WIDTH), lambda i: (i, col_block)),
                  pl.BlockSpec((halo, POOL_WIDTH),
                               lambda i: (jnp.maximum(i * r - 1, 0), col_block)),
                  pl.BlockSpec((ng, POOL_GROUP_DIM, POOL_GROUP_DIM), lambda i: (0, 0, 0)),
                  pl.BlockSpec((1, POOL_WIDTH), lambda i: (0, 0))],
        out_specs=pl.BlockSpec((tm, POOL_WIDTH), lambda i: (i, 0)),
        out_shape=jax.ShapeDtypeStruct((t, POOL_WIDTH), BF16),
        compiler_params=_params(("arbitrary",), _vmem_estimate(
            [((tm, POOL_WIDTH), BF16), ((halo, POOL_WIDTH), BF16),
             ((ng, POOL_GROUP_DIM, POOL_GROUP_DIM), BF16), ((tm, POOL_WIDTH), BF16)],
            temps=[((tm + halo, POOL_GROUP_DIM), F32)] * 6)),
        name="pool_mixer",
    )(proj, proj, pool_w, pool_scale)


def _attn_kernel(q_ref, k_ref, v_ref, o_ref, vt_ref, acc_ref, *, tk, unroll, heads):
    qi = pl.program_id(1)
    tq = q_ref.shape[0]
    grp = tk * unroll
    assert grp == tq
    ngrp = k_ref.shape[0] // grp
    qscale = HEAD_DIM ** -0.5 * LOG2E

    def lanes(h):
        return slice(h * HEAD_DIM, (h + 1) * HEAD_DIM)

    @pl.when(qi == 0)
    def _():
        def xpose(g, _):
            rows = pl.ds(pl.multiple_of(g * grp, grp), grp)
            for h in range(heads):
                vt_ref[h, g] = v_ref[rows, lanes(h)].astype(F32).T.astype(BF16)
            return 0
        lax.fori_loop(0, ngrp, xpose, 0)

    qs = [(q_ref[:, lanes(h)].astype(F32) * qscale).astype(BF16) for h in range(heads)]
    rr = lax.broadcasted_iota(jnp.int32, (tk, 2 * tk), 0)
    cc = lax.broadcasted_iota(jnp.int32, (tk, 2 * tk), 1)
    tri2 = jnp.where((cc % tk) >= rr, 1.0, 0.0).astype(BF16)

    def sweep(groups, cs, init=False):
        hs = range(heads)
        valid = None
        if any(masked for _, masked, _ in groups):
            valid = (lax.broadcasted_iota(jnp.int32, (grp, tq), 0)
                     < lax.broadcasted_iota(jnp.int32, (grp, tq), 1))
        zs = []
        for g, _, _ in groups:
            rows_g = pl.ds(pl.multiple_of(g * grp, grp), grp)
            zs.append([lax.dot_general(k_ref[rows_g, lanes(h)], qs[h],
                                       (((1,), (1,)), ((), ())),
                                       preferred_element_type=F32) for h in hs])
        hls = []
        for n, (_, masked, _) in enumerate(groups):
            hls.append([])
            for h in hs:
                z = zs[n][h]
                p = jnp.maximum(z, 0.0) + jnp.log(1.0 + jnp.exp2(-jnp.abs(z))) * LOG2E
                if masked:
                    p = jnp.where(valid, p, 0.0)
                hi = p.astype(BF16)
                lo = (p - hi.astype(F32)).astype(BF16)
                hls[n].append((hi, lo))
        sts = [[[None] * unroll for _ in hs] for _ in groups]
        for n in range(len(groups)):
            for u in reversed(range(unroll)):
                rows = slice(u * tk, (u + 1) * tk)
                for h in hs:
                    hi, lo = hls[n][h]
                    sts[n][h][u] = jnp.dot(
                        tri2, jnp.concatenate([hi[rows], lo[rows]], axis=0),
                        preferred_element_type=F32)
        cs = list(cs)
        wss = []
        for n, (_, masked, live) in enumerate(groups):
            wss.append([])
            for h in hs:
                ws = [None] * unroll
                for u in reversed(range(unroll)):
                    rows = slice(u * tk, (u + 1) * tk)
                    w = jnp.exp2(zs[n][h][rows] - sts[n][h][u] - cs[h])
                    total = sts[n][h][u][0:1, :]
                    if masked:
                        w = jnp.where(valid[rows], w, 0.0)
                    if live is not None:
                        w = jnp.where(live, w, 0.0)
                        total = jnp.where(live, total, 0.0)
                    ws[u] = w.astype(BF16)
                    cs[h] = cs[h] + total
                wss[n].append(jnp.concatenate(ws, axis=0))
        for h in hs:
            vt = jnp.concatenate([vt_ref[h, g] for g, _, _ in groups], axis=1)
            wv = jnp.dot(vt, jnp.concatenate([ws[h] for ws in wss], axis=0),
                         preferred_element_type=F32)
            if init:
                acc_ref[h] = wv
            else:
                acc_ref[h] += wv
        return tuple(cs)

    def smallest(cs):
        return jnp.min(functools.reduce(jnp.minimum, cs))

    def more(state):
        return jnp.logical_and(state[0] < qi, state[1] < ATTN_SKIP_LOG2)

    def older(state):
        it, cs = state[0], sweep([(qi - 1 - state[0], False, None)], state[2:])
        return (it + 1, smallest(cs)) + cs

    cs = sweep([(qi, True, None), (jnp.maximum(qi - 1, 0), False, qi > 0)],
               (jnp.zeros((1, tq), F32),) * heads, init=True)
    lax.while_loop(more, older, (jnp.int32(1), smallest(cs)) + cs)
    for h in range(heads):
        o_ref[:, lanes(h)] = acc_ref[h].T.astype(o_ref.dtype)


def _attention(proj, batch, seq, tk=ATTN_TK, unroll=ATTN_UNROLL, heads=ATTN_HEADS_PER_STEP):
    t = proj.shape[0]
    tq = tk * unroll
    nq = seq // tq
    hgroups = HEADS // heads
    width = heads * HEAD_DIM

    def bh(g):
        return g // hgroups, g % hgroups

    def q_map(g, qi):
        b, h = bh(g)
        return b * nq + qi, h

    def k_map(g, qi):
        b, h = bh(g)
        return b, hgroups + h

    def v_map(g, qi):
        b, h = bh(g)
        return b, 2 * hgroups + h

    return pl.pallas_call(
        functools.partial(_attn_kernel, tk=tk, unroll=unroll, heads=heads),
        grid=(batch * hgroups, nq),
        in_specs=[pl.BlockSpec((tq, width), q_map),
                  pl.BlockSpec((seq, width), k_map, pipeline_mode=pl.Buffered(1)),
                  pl.BlockSpec((seq, width), v_map, pipeline_mode=pl.Buffered(1))],
        out_specs=pl.BlockSpec((tq, width), q_map),
        out_shape=jax.ShapeDtypeStruct((t, SB_WIDTH), BF16),
        scratch_shapes=[pltpu.VMEM((heads, nq, HEAD_DIM, tq), BF16),
                        pltpu.VMEM((heads, HEAD_DIM, tq), F32)],
        compiler_params=_params(("arbitrary", "arbitrary"), _vmem_estimate(
            [((tq, width), BF16), ((tq, width), BF16)],
            scratch=[((seq, width), BF16)] * 3 + [((width, tq), F32)],
            temps=[((tq, tq), F32)] * 4 * heads)),
        name="stickbreak_attn",
    )(proj, proj, proj)


def _cast_pad_kernel(x_ref, o_ref, *, tiles_in, tiles_out):
    tile = pl.program_id(0) % tiles_out
    o_ref[...] = jnp.where(tile < tiles_in, x_ref[...].astype(o_ref.dtype),
                           jnp.zeros((), o_ref.dtype))


def _cast_pad(a, dtype, axis):
    tile = math.gcd(D_FF, D_FF_PAD)
    tiles_in, tiles_out = D_FF // tile, D_FF_PAD // tile
    sections = a.shape[axis] // D_FF
    other = a.shape[1 - axis]

    def src(c):
        return (c // tiles_out) * tiles_in + jnp.minimum(c % tiles_out, tiles_in - 1)

    if axis == 1:
        block, in_map, out_map = (other, tile), (lambda c: (0, src(c))), (lambda c: (0, c))
        out_shape = (other, sections * D_FF_PAD)
    else:
        block, in_map, out_map = (tile, other), (lambda c: (src(c), 0)), (lambda c: (c, 0))
        out_shape = (sections * D_FF_PAD, other)
    return pl.pallas_call(
        functools.partial(_cast_pad_kernel, tiles_in=tiles_in, tiles_out=tiles_out),
        grid=(sections * tiles_out,),
        in_specs=[pl.BlockSpec(block, in_map)],
        out_specs=pl.BlockSpec(block, out_map),
        out_shape=jax.ShapeDtypeStruct(out_shape, dtype),
        compiler_params=_params(("arbitrary",), _vmem_estimate(
            [(block, a.dtype), (block, dtype)], temps=[(block, a.dtype)])),
        name="cast_pad",
    )(a)


def _pad_halves(a):
    pad = lambda h: jnp.pad(h, ((0, 0), (0, D_FF_PAD - D_FF)))
    return jnp.concatenate([pad(a[:, :D_FF]), pad(a[:, D_FF:])], axis=1)


def _layer(x, pre_mix_norm, w_in, pool_w, pool_scale, w_out, post_mix_norm,
           pre_ffn_norm, w_up, conv_w, conv_b, w_down, post_ffn_norm, batch, seq):
    vec = lambda g: g.reshape(1, -1).astype(F32)
    w_in_b = w_in.astype(BF16)
    w_out_b = w_out.astype(BF16)
    pool_w_b = pool_w.astype(BF16)
    w_up_b = _cast_pad(w_up, BF16, axis=1)
    conv_w_p = _pad_halves(conv_w)
    conv_b_p = _pad_halves(conv_b[None, :])
    w_down_b = _cast_pad(w_down, BF16, axis=0)

    h = _prenorm(x, vec(pre_mix_norm))
    proj = _in_proj(h, w_in_b)
    attn = _attention(proj, batch, seq)
    pooled = _pool(proj, pool_w_b, vec(pool_scale), seq)
    mix = _out_proj(attn, pooled, w_out_b)
    x1, h2 = _post_mix(x, mix, vec(post_mix_norm), vec(pre_ffn_norm))
    act = _ffn_up(h2, w_up_b, conv_w_p, conv_b_p, seq)
    ffn = _down_proj(act, w_down_b)
    return _post_ffn(x1, ffn, vec(post_ffn_norm))


def kernel(x, pre_mix_norm, w_in, pool_w, pool_scale, w_out, post_mix_norm,
           pre_ffn_norm, w_up, conv_w, conv_b, w_down, post_ffn_norm):
    batch, seq, d = x.shape
    assert d == D_MODEL and w_up.shape[-1] == 2 * D_FF
    xt = x.reshape(batch * seq, d)
    for l in range(pre_mix_norm.shape[0]):
        xt = _layer(xt, pre_mix_norm[l], w_in[l], pool_w[l], pool_scale[l], w_out[l],
                    post_mix_norm[l], pre_ffn_norm[l], w_up[l], conv_w[l], conv_b[l],
                    w_down[l], post_ffn_norm[l], batch, seq)
    return xt.reshape(batch, seq, d)
```
